```python
import math
import jax, jax.numpy as jnp
from jax import lax
import numpy as np

D_MODEL = 1024
BATCH = 4
SEQ = 4096
DEPTH = 2

N_BRANCH = 4
MIX_WIDTH = D_MODEL // 4
S5_GROUP = 16
S5_GROUPS = MIX_WIDTH // S5_GROUP
S5_STATE = 64
S5_DT_MIN = 1e-3
S5_DT_MAX = 1e-1
RWKV_HEAD = 64
RWKV_HEADS = MIX_WIDTH // RWKV_HEAD
RWKV_DECAY_LORA = 32
RWKV_A_LORA = 32
RWKV_GATE_LORA = 64
RWKV_GN_EPS = 64e-5
RWKV_WIDTH = 3 * MIX_WIDTH + RWKV_DECAY_LORA + RWKV_A_LORA + RWKV_GATE_LORA
GLA_HEADS = 4
GLA_KEY = MIX_WIDTH // 2
GLA_DK = GLA_KEY // GLA_HEADS
GLA_DV = MIX_WIDTH // GLA_HEADS
GLA_GATE_LORA = 16
GLA_GATE_NORM = 16.0
GLA_CHUNK = 32
GLA_WIDTH = 2 * GLA_KEY + 2 * MIX_WIDTH + GLA_GATE_LORA
RG_BLOCKS = 4
RG_BLOCK = MIX_WIDTH // RG_BLOCKS
RG_CONV = 4
RG_C = 8.0
RG_WIDTH = 2 * MIX_WIDTH
GATE_WIDTH = N_BRANCH * D_MODEL
IN_WIDTH = MIX_WIDTH + RWKV_WIDTH + GLA_WIDTH + RG_WIDTH + GATE_WIDTH
FFN_DIM = ((8 * D_MODEL // 3 + 127) // 128) * 128
FFN_CONV = 3
NORM_EPS = 1e-6

kernel_name = 'hybrid_s5_rwkv7_gla_rglru_block'


def rms_norm(x, w):
    xf = x.astype(jnp.float32)
    y = xf * lax.rsqrt(jnp.mean(xf * xf, axis=-1, keepdims=True) + NORM_EPS)
    return (y * w.astype(jnp.float32)).astype(x.dtype)


def split_cols(h, sizes):
    idx = np.cumsum(sizes)[:-1].tolist()
    return jnp.split(h, idx, axis=-1)


def causal_dwconv(x, w, b):
    width, ch = w.shape
    y = lax.conv_general_dilated(
        x, w[:, None, :].astype(x.dtype), window_strides=(1,),
        padding=((width - 1, 0),), dimension_numbers=('NWC', 'WIO', 'NWC'),
        feature_group_count=ch)
    return y + b


def token_shift(x):
    return jnp.pad(x, ((0, 0), (1, 0), (0, 0)))[:, :-1]


def _complex_affine_combine(e1, e2):
    ar1, ai1, br1, bi1 = e1
    ar2, ai2, br2, bi2 = e2
    return (ar1 * ar2 - ai1 * ai2, ar1 * ai2 + ai1 * ar2,
            ar2 * br1 - ai2 * bi1 + br2, ar2 * bi1 + ai2 * br1 + bi2)


def _real_affine_combine(e1, e2):
    return (e1[0] * e2[0], e2[0] * e1[1] + e2[1])


def s5_mixer(u, lam_re, lam_im, log_dt, b_re, b_im, c_re, c_im, d, w_glu, b_glu):
    bsz, L, _ = u.shape
    uf = u.astype(jnp.float32).reshape(bsz, L, S5_GROUPS, S5_GROUP)
    lr = lam_re.astype(jnp.float32)
    li = lam_im.astype(jnp.float32)
    dt = jnp.exp(log_dt.astype(jnp.float32))[:, None]
    mag = jnp.exp(lr * dt)
    ang = li * dt
    ab_re = mag * jnp.cos(ang)
    ab_im = mag * jnp.sin(ang)
    den = lr * lr + li * li
    f_re = ((ab_re - 1.0) * lr + ab_im * li) / den
    f_im = (ab_im * lr - (ab_re - 1.0) * li) / den
    br = b_re.astype(jnp.float32)
    bi = b_im.astype(jnp.float32)
    bb_re = f_re[:, :, None] * br - f_im[:, :, None] * bi
    bb_im = f_re[:, :, None] * bi + f_im[:, :, None] * br
    bu_re = jnp.einsum('gpc,blgc->blgp', bb_re, uf)
    bu_im = jnp.einsum('gpc,blgc->blgp', bb_im, uf)
    a_re = jnp.broadcast_to(ab_re, bu_re.shape)
    a_im = jnp.broadcast_to(ab_im, bu_im.shape)
    _, _, h_re, h_im = lax.associative_scan(
        _complex_affine_combine, (a_re, a_im, bu_re, bu_im), axis=1)
    y = (jnp.einsum('gcp,blgp->blgc', c_re.astype(jnp.float32), h_re)
         - jnp.einsum('gcp,blgp->blgc', c_im.astype(jnp.float32), h_im))
    y = y + d.astype(jnp.float32) * uf
    y = jax.nn.gelu(y.reshape(bsz, L, MIX_WIDTH))
    y = y * jax.nn.sigmoid(y @ w_glu.astype(jnp.float32) + b_glu.astype(jnp.float32))
    return y.astype(u.dtype)


def rwkv7_mixer(p, mu, w0, w_up, a0, a_up, g_up, k_k, k_a, r_k, ln_w, ln_b):
    bsz, L, _ = p.shape
    p = p + (token_shift(p) - p) * mu
    r, k, v, wd, ad, gd = split_cols(
        p, [MIX_WIDTH, MIX_WIDTH, MIX_WIDTH, RWKV_DECAY_LORA, RWKV_A_LORA, RWKV_GATE_LORA])
    w = -jax.nn.softplus(-(w0 + jnp.tanh(wd) @ w_up)) - 0.5
    decay = jnp.exp(-jnp.exp(w.astype(jnp.float32)))
    a = jax.nn.sigmoid(a0 + ad @ a_up)
    g = jax.nn.sigmoid(gd) @ g_up

    def heads(t):
        return t.astype(jnp.float32).reshape(bsz, L, RWKV_HEADS, RWKV_HEAD)

    kk = heads(k * k_k)
    kk = kk / jnp.maximum(jnp.sqrt(jnp.sum(kk * kk, axis=-1, keepdims=True)), 1e-12)
    k = k * (1.0 + (a - 1.0) * k_a)
    rh, kh, vh, ah, dh = heads(r), heads(k), heads(v), heads(a), heads(decay)
    avec = -kk
    bvec = kk * ah

    def step(S, inp):
        r_t, k_t, v_t, a_t, b_t, w_t = inp
        sa = jnp.einsum('bhvk,bhk->bhv', S, a_t)
        S = (S * w_t[:, :, None, :] + sa[..., None] * b_t[:, :, None, :]
             + v_t[..., None] * k_t[:, :, None, :])
        return S, jnp.einsum('bhvk,bhk->bhv', S, r_t)

    S0 = jnp.zeros((bsz, RWKV_HEADS, RWKV_HEAD, RWKV_HEAD), jnp.float32)
    xs = tuple(jnp.moveaxis(t, 1, 0) for t in (rh, kh, vh, avec, bvec, dh))
    _, o = lax.scan(step, S0, xs)
    o = jnp.moveaxis(o, 0, 1)
    mean = jnp.mean(o, axis=-1, keepdims=True)
    var = jnp.mean(jnp.square(o - mean), axis=-1, keepdims=True)
    o = (o - mean) * lax.rsqrt(var + RWKV_GN_EPS)
    o = o.reshape(bsz, L, MIX_WIDTH) * ln_w.astype(jnp.float32) + ln_b.astype(jnp.float32)
    bonus = jnp.sum(rh * kh * r_k.astype(jnp.float32), axis=-1, keepdims=True) * vh
    o = (o + bonus.reshape(bsz, L, MIX_WIDTH)) * g.astype(jnp.float32)
    return o.astype(p.dtype)


def gla_mixer(q, k, v, g, ad, alpha_up, alpha_b, norm_w):
    bsz, L, _ = q.shape
    nc = L // GLA_CHUNK
    gk = jax.nn.log_sigmoid((ad @ alpha_up + alpha_b).astype(jnp.float32)) / GLA_GATE_NORM

    def chunks(t, dh):
        return t.astype(jnp.float32).reshape(bsz, nc, GLA_CHUNK, GLA_HEADS, dh).transpose(0, 3, 1, 2, 4)

    qc = chunks(q, GLA_DK) * (GLA_DK ** -0.5)
    kc = chunks(k, GLA_DK)
    vc = chunks(v, GLA_DV)
    bc = jnp.cumsum(chunks(gk, GLA_DK), axis=3)
    causal = jnp.tril(jnp.ones((GLA_CHUNK, GLA_CHUNK), dtype=bool))
    diff = bc[:, :, :, :, None, :] - bc[:, :, :, None, :, :]
    pair_decay = jnp.exp(jnp.where(causal[:, :, None], diff, -jnp.inf))
    attn = jnp.einsum('bhnid,bhnjd,bhnijd->bhnij', qc, kc, pair_decay)
    o_intra = jnp.einsum('bhnij,bhnjv->bhniv', attn, vc)
    b_last = bc[:, :, :, -1:, :]
    d_state = jnp.einsum('bhnjd,bhnjv->bhndv', kc * jnp.exp(b_last - bc), vc)
    chunk_decay = jnp.exp(b_last[:, :, :, 0, :])

    def step(S, inp):
        dec, ds = inp
        return dec[..., None] * S + ds, S

    S0 = jnp.zeros((bsz, GLA_HEADS, GLA_DK, GLA_DV), jnp.float32)
    _, s_prev = lax.scan(step, S0, (jnp.moveaxis(chunk_decay, 2, 0), jnp.moveaxis(d_state, 2, 0)))
    s_prev = jnp.moveaxis(s_prev, 0, 2)
    o_inter = jnp.einsum('bhnid,bhndv->bhniv', qc * jnp.exp(bc), s_prev)
    o = (o_intra + o_inter).transpose(0, 2, 3, 1, 4).reshape(bsz, L, GLA_HEADS, GLA_DV)
    o = o * lax.rsqrt(jnp.mean(o * o, axis=-1, keepdims=True) + NORM_EPS)
    o = o.reshape(bsz, L, MIX_WIDTH) * norm_w.astype(jnp.float32) * jax.nn.silu(g.astype(jnp.float32))
    return o.astype(q.dtype)


def rglru_mixer(xb, yb, conv_w, conv_b, w_a, b_a, w_x, b_x, lam):
    bsz, L, _ = xb.shape
    xc = causal_dwconv(xb, conv_w, conv_b)
    xblk = xc.reshape(bsz, L, RG_BLOCKS, RG_BLOCK)
    gate_r = jax.nn.sigmoid(jnp.einsum('blhi,hij->blhj', xblk, w_a).reshape(bsz, L, MIX_WIDTH) + b_a)
    gate_i = jax.nn.sigmoid(jnp.einsum('blhi,hij->blhj', xblk, w_x).reshape(bsz, L, MIX_WIDTH) + b_x)
    log_a = (-RG_C * gate_r * jax.nn.softplus(-lam)).astype(jnp.float32)
    a = jnp.exp(log_a)
    mult = jnp.sqrt(-jnp.expm1(2.0 * log_a))
    mult = jnp.where((jnp.arange(L) == 0)[None, :, None], 1.0, mult)
    bx = mult * (gate_i * xc).astype(jnp.float32)
    _, h = lax.associative_scan(_real_affine_combine, (a, bx), axis=1)
    return (h * jax.nn.gelu(yb.astype(jnp.float32))).astype(xb.dtype)


def hybrid_mixer(h, w_in,
                 s5_lambda_re, s5_lambda_im, s5_log_dt, s5_b_re, s5_b_im, s5_c_re, s5_c_im,
                 s5_d, s5_w_glu, s5_b_glu,
                 rwkv_mu, rwkv_w0, rwkv_w_up, rwkv_a0, rwkv_a_up, rwkv_g_up, rwkv_k_k,
                 rwkv_k_a, rwkv_r_k, rwkv_ln_w, rwkv_ln_b,
                 gla_alpha_up, gla_alpha_b, gla_norm_w,
                 rg_conv_w, rg_conv_b, rg_w_a, rg_b_a, rg_w_x, rg_b_x, rg_lambda,
                 w_branch, w_out):
    bsz, L, _ = h.shape
    p = h @ w_in
    (u_s5, p_rwkv, g_q, g_k, g_v, g_g, g_ad, r_x, r_y, gates) = split_cols(
        p, [MIX_WIDTH, RWKV_WIDTH, GLA_KEY, GLA_KEY, MIX_WIDTH, MIX_WIDTH, GLA_GATE_LORA,
            MIX_WIDTH, MIX_WIDTH, GATE_WIDTH])
    y_a = s5_mixer(u_s5, s5_lambda_re, s5_lambda_im, s5_log_dt, s5_b_re, s5_b_im,
                   s5_c_re, s5_c_im, s5_d, s5_w_glu, s5_b_glu)
    y_b = rwkv7_mixer(p_rwkv, rwkv_mu, rwkv_w0, rwkv_w_up, rwkv_a0, rwkv_a_up, rwkv_g_up,
                      rwkv_k_k, rwkv_k_a, rwkv_r_k, rwkv_ln_w, rwkv_ln_b)
    y_c = gla_mixer(g_q, g_k, g_v, g_g, g_ad, gla_alpha_up, gla_alpha_b, gla_norm_w)
    y_d = rglru_mixer(r_x, r_y, rg_conv_w, rg_conv_b, rg_w_a, rg_b_a, rg_w_x, rg_b_x, rg_lambda)
    ys = jnp.stack([y_a, y_b, y_c, y_d], axis=2)
    proj = jnp.einsum('blkc,kcd->blkd', ys, w_branch)
    gate = jax.nn.sigmoid(gates.reshape(bsz, L, N_BRANCH, D_MODEL))
    merged = jnp.sum(gate * proj, axis=2)
    return merged @ w_out


def conv_ffn(h, w_in, conv_w, conv_b, w_out):
    gate, val = jnp.split(h @ w_in, 2, axis=-1)
    gate = causal_dwconv(gate, conv_w, conv_b)
    return (jax.nn.gelu(gate) * val) @ w_out


def setup_inputs(seed: int = 0) -> dict:
    key = jax.random.key(seed)
    ks = iter(jax.random.split(key, 64))

    def nrm(shape, scale):
        return scale * jax.random.normal(next(ks), shape, jnp.float32)

    def unif(shape, lo, hi):
        return jax.random.uniform(next(ks), shape, jnp.float32, lo, hi)

    Ld = DEPTH
    G, P, Hg = S5_GROUPS, S5_STATE, S5_GROUP
    s5_lambda_re = -0.5 + nrm((Ld, G, P), 0.01)
    s5_lambda_im = math.pi * jnp.arange(P, dtype=jnp.float32)[None, None, :] + nrm((Ld, G, P), 0.01)
    rg_a8 = unif((Ld, MIX_WIDTH), 0.9, 0.999)
    rg_a = rg_a8 ** (1.0 / RG_C)
    rg_lambda = jnp.log(rg_a) - jnp.log1p(-rg_a)
    return {
        'x': nrm((BATCH, SEQ, D_MODEL), 1.0),
        'norm_mix_pre': 1.0 + nrm((Ld, D_MODEL), 0.01),
        'norm_mix_post': 1.0 + nrm((Ld, D_MODEL), 0.01),
        'norm_ffn_pre': 1.0 + nrm((Ld, D_MODEL), 0.01),
        'norm_ffn_post': 1.0 + nrm((Ld, D_MODEL), 0.01),
        'w_in': nrm((Ld, D_MODEL, IN_WIDTH), D_MODEL ** -0.5),
        's5_lambda_re': s5_lambda_re,
        's5_lambda_im': s5_lambda_im,
        's5_log_dt': unif((Ld, G), math.log(S5_DT_MIN), math.log(S5_DT_MAX)),
        's5_b_re': nrm((Ld, G, P, Hg), (2 * Hg) ** -0.5),
        's5_b_im': nrm((Ld, G, P, Hg), (2 * Hg) ** -0.5),
        's5_c_re': nrm((Ld, G, Hg, P), P ** -0.5),
        's5_c_im': nrm((Ld, G, Hg, P), P ** -0.5),
        's5_d': nrm((Ld, G, Hg), 1.0),
        's5_w_glu': nrm((Ld, MIX_WIDTH, MIX_WIDTH), MIX_WIDTH ** -0.5),
        's5_b_glu': nrm((Ld, MIX_WIDTH), 0.01),
        'rwkv_mu': unif((Ld, RWKV_WIDTH), 0.0, 1.0),
        'rwkv_w0': unif((Ld, MIX_WIDTH), -6.0, -1.0),
        'rwkv_w_up': nrm((Ld, RWKV_DECAY_LORA, MIX_WIDTH), 0.1),
        'rwkv_a0': nrm((Ld, MIX_WIDTH), 0.1),
        'rwkv_a_up': nrm((Ld, RWKV_A_LORA, MIX_WIDTH), 0.1),
        'rwkv_g_up': nrm((Ld, RWKV_GATE_LORA, MIX_WIDTH), RWKV_GATE_LORA ** -0.5),
        'rwkv_k_k': 0.85 + nrm((Ld, MIX_WIDTH), 0.02),
        'rwkv_k_a': 1.0 + nrm((Ld, MIX_WIDTH), 0.02),
        'rwkv_r_k': nrm((Ld, RWKV_HEADS, RWKV_HEAD), 0.1),
        'rwkv_ln_w': 1.0 + nrm((Ld, MIX_WIDTH), 0.01),
        'rwkv_ln_b': nrm((Ld, MIX_WIDTH), 0.01),
        'gla_alpha_up': nrm((Ld, GLA_GATE_LORA, GLA_KEY), GLA_GATE_LORA ** -0.5),
        'gla_alpha_b': nrm((Ld, GLA_KEY), 0.1),
        'gla_norm_w': 1.0 + nrm((Ld, MIX_WIDTH), 0.01),
        'rg_conv_w': nrm((Ld, RG_CONV, MIX_WIDTH), RG_CONV ** -0.5),
        'rg_conv_b': nrm((Ld, MIX_WIDTH), 0.01),
        'rg_w_a': nrm((Ld, RG_BLOCKS, RG_BLOCK, RG_BLOCK), RG_BLOCK ** -0.5),
        'rg_b_a': nrm((Ld, MIX_WIDTH), 0.01),
        'rg_w_x': nrm((Ld, RG_BLOCKS, RG_BLOCK, RG_BLOCK), RG_BLOCK ** -0.5),
        'rg_b_x': nrm((Ld, MIX_WIDTH), 0.01),
        'rg_lambda': rg_lambda,
        'w_branch': nrm((Ld, N_BRANCH, MIX_WIDTH, D_MODEL), MIX_WIDTH ** -0.5),
        'w_out': nrm((Ld, D_MODEL, D_MODEL), D_MODEL ** -0.5),
        'ffn_w_in': nrm((Ld, D_MODEL, 2 * FFN_DIM), D_MODEL ** -0.5),
        'ffn_conv_w': nrm((Ld, FFN_CONV, FFN_DIM), FFN_CONV ** -0.5),
        'ffn_conv_b': nrm((Ld, FFN_DIM), 0.01),
        'ffn_w_out': nrm((Ld, FFN_DIM, D_MODEL), FFN_DIM ** -0.5),
    }


def reference(x, norm_mix_pre, norm_mix_post, norm_ffn_pre, norm_ffn_post, w_in,
              s5_lambda_re, s5_lambda_im, s5_log_dt, s5_b_re, s5_b_im, s5_c_re, s5_c_im,
              s5_d, s5_w_glu, s5_b_glu,
              rwkv_mu, rwkv_w0, rwkv_w_up, rwkv_a0, rwkv_a_up, rwkv_g_up, rwkv_k_k,
              rwkv_k_a, rwkv_r_k, rwkv_ln_w, rwkv_ln_b,
              gla_alpha_up, gla_alpha_b, gla_norm_w,
              rg_conv_w, rg_conv_b, rg_w_a, rg_b_a, rg_w_x, rg_b_x, rg_lambda,
              w_branch, w_out, ffn_w_in, ffn_conv_w, ffn_conv_b, ffn_w_out):
    for l in range(DEPTH):
        h = rms_norm(x, norm_mix_pre[l])
        m = hybrid_mixer(
            h, w_in[l],
            s5_lambda_re[l], s5_lambda_im[l], s5_log_dt[l], s5_b_re[l], s5_b_im[l],
            s5_c_re[l], s5_c_im[l], s5_d[l], s5_w_glu[l], s5_b_glu[l],
            rwkv_mu[l], rwkv_w0[l], rwkv_w_up[l], rwkv_a0[l], rwkv_a_up[l], rwkv_g_up[l],
            rwkv_k_k[l], rwkv_k_a[l], rwkv_r_k[l], rwkv_ln_w[l], rwkv_ln_b[l],
            gla_alpha_up[l], gla_alpha_b[l], gla_norm_w[l],
            rg_conv_w[l], rg_conv_b[l], rg_w_a[l], rg_b_a[l], rg_w_x[l], rg_b_x[l], rg_lambda[l],
            w_branch[l], w_out[l])
        x = x + rms_norm(m, norm_mix_post[l]).astype(x.dtype)
        h = rms_norm(x, norm_ffn_pre[l])
        f = conv_ffn(h, ffn_w_in[l], ffn_conv_w[l], ffn_conv_b[l], ffn_w_out[l])
        x = x + rms_norm(f, norm_ffn_post[l]).astype(x.dtype)
    return x
```

```python
import functools
import math

import numpy as np
import jax
import jax.numpy as jnp
from jax import lax
from jax.experimental import pallas as pl
from jax.experimental.pallas import tpu as pltpu

F32 = jnp.float32
BF16 = jnp.bfloat16

D_MODEL = 1024
MIX = 256
NORM_EPS = 1e-6
S5_GROUPS, S5_GROUP, S5_STATE = 16, 16, 64
S5_NSTATE = S5_GROUPS * S5_STATE
S5_DT_MIN, S5_DT_MAX = 1e-3, 1e-1
RWKV_HEADS, RWKV_HEAD = 4, 64
RWKV_WIDTH = 3 * MIX + 128
RWKV_GN_EPS = 64e-5
GLA_HEADS, GLA_DK, GLA_DV = 4, 32, 64
GLA_KEY = GLA_HEADS * GLA_DK
GLA_GATE_LORA = 16
GLA_GATE_NORM = 16.0
GLA_WIDTH = 2 * GLA_KEY + 2 * MIX + GLA_GATE_LORA
GLA_WIDTH_PAD = 896
RG_CONV = 4
RG_C = 8.0
FFN_DIM = 2816
FFN_CONV = 3
HALO = 8

S5_T = 128
RG_T = 128
GLA_T = 128
RWKV_C = 64
PROJ_T = 512
MERGE_T = 256
FFN_T = 256

VMEM_LIMIT = 56 * 1024 * 1024


def _dot(a, b):
    return jnp.dot(a.astype(BF16), b.astype(BF16), preferred_element_type=F32)


def _dot_nt(a, b):
    return lax.dot_general(a.astype(BF16), b.astype(BF16), (((1,), (1,)), ((), ())),
                           preferred_element_type=F32)


def _dot_tn(a, b):
    return lax.dot_general(a.astype(BF16), b.astype(BF16), (((0,), (0,)), ((), ())),
                           preferred_element_type=F32)


def _split3(x):
    hi = x.astype(BF16)
    r1 = x - hi.astype(F32)
    mid = r1.astype(BF16)
    lo = (r1 - mid.astype(F32)).astype(BF16)
    return hi, mid, lo


def _sel_dot(c, x):
    hi, mid, lo = _split3(x)
    return (jnp.dot(c, hi, preferred_element_type=F32)
            + jnp.dot(c, mid, preferred_element_type=F32)
            + jnp.dot(c, lo, preferred_element_type=F32))


def _seg_dot(x, j):
    hi, mid, lo = _split3(x)
    return (jnp.dot(hi, j, preferred_element_type=F32)
            + jnp.dot(mid, j, preferred_element_type=F32)
            + jnp.dot(lo, j, preferred_element_type=F32))


def _rms(x, w):
    return x * lax.rsqrt(jnp.mean(x * x, axis=-1, keepdims=True) + NORM_EPS) * w


def _softplus(y):
    return jnp.maximum(y, 0.0) + jnp.log1p(jnp.exp(-jnp.abs(y)))


def _shift_rows(x, s, fill):
    rolled = pltpu.roll(x, s, axis=0)
    row = lax.broadcasted_iota(jnp.int32, x.shape, 0)
    return jnp.where(row >= s, rolled, fill)


def _stack_heads(x, hm_ref, nheads):
    return jnp.concatenate([x * hm_ref[h:h + 1, :] for h in range(nheads)], axis=0)


def _unstack_heads(x, nheads):
    c = x.shape[0] // nheads
    out = x[0:c]
    for h in range(1, nheads):
        out = out + x[h * c:(h + 1) * c]
    return out


def _const_spec(shape):
    nd = len(shape)
    return pl.BlockSpec(shape, lambda *_: (0,) * nd)


def _params(sem):
    return pltpu.CompilerParams(dimension_semantics=sem, vmem_limit_bytes=VMEM_LIMIT)


def _proj_kernel(x_ref, nw_ref, *refs):
    n = len(refs) // 2
    h = _rms(x_ref[...], nw_ref[...]).astype(BF16)
    for w_ref, o_ref in zip(refs[:n], refs[n:]):
        o_ref[...] = jnp.dot(h, w_ref[...], preferred_element_type=F32)


def _proj(x2, nw, ws):
    n_tok = x2.shape[0]
    widths = [w.shape[1] for w in ws]
    return pl.pallas_call(
        _proj_kernel,
        grid=(n_tok // PROJ_T,),
        in_specs=[pl.BlockSpec((PROJ_T, D_MODEL), lambda i: (i, 0)), _const_spec((1, D_MODEL))]
        + [_const_spec((D_MODEL, wd)) for wd in widths],
        out_specs=[pl.BlockSpec((PROJ_T, wd), lambda i: (i, 0)) for wd in widths],
        out_shape=[jax.ShapeDtypeStruct((n_tok, wd), F32) for wd in widths],
        compiler_params=_params(("parallel",)),
        name="proj",
    )(x2, nw, *ws)


def _s5_kernel(u_ref, bmat_ref, pwre_ref, pwim_ref, cmat_ref, d_ref, wglu_ref, bglu_ref,
               o_ref, carry_ref):
    @pl.when(pl.program_id(1) == 0)
    def _():
        carry_ref[...] = jnp.zeros_like(carry_ref)

    u = u_ref[0]
    bu = _dot(u, bmat_ref[...])
    hre = bu[:, :S5_NSTATE]
    him = bu[:, S5_NSTATE:]
    s = 1
    while s < S5_T:
        ar = pwre_ref[s - 1:s, :]
        ai = pwim_ref[s - 1:s, :]
        sre = _shift_rows(hre, s, 0.0)
        sim = _shift_rows(him, s, 0.0)
        hre, him = hre + ar * sre - ai * sim, him + ar * sim + ai * sre
        s *= 2
    cre = carry_ref[0:1, :]
    cim = carry_ref[1:2, :]
    pre = pwre_ref[...]
    pim = pwim_ref[...]
    hre, him = hre + pre * cre - pim * cim, him + pre * cim + pim * cre
    carry_ref[0:1, :] = hre[S5_T - 1:S5_T, :]
    carry_ref[1:2, :] = him[S5_T - 1:S5_T, :]
    y = _dot(jnp.concatenate([hre, him], axis=1), cmat_ref[...]) + d_ref[...] * u
    y = jax.nn.gelu(y)
    o_ref[0] = y * jax.nn.sigmoid(_dot(y, wglu_ref[...]) + bglu_ref[...])


def _s5(u, lam_re, lam_im, log_dt, b_re, b_im, c_re, c_im, d, w_glu, b_glu):
    bsz, seq, _ = u.shape
    dt = jnp.exp(log_dt)[:, None]
    mag = jnp.exp(lam_re * dt)
    ang = lam_im * dt
    ab_re = mag * jnp.cos(ang)
    ab_im = mag * jnp.sin(ang)
    den = lam_re * lam_re + lam_im * lam_im
    f_re = ((ab_re - 1.0) * lam_re + ab_im * lam_im) / den
    f_im = (ab_im * lam_re - (ab_re - 1.0) * lam_im) / den
    bb_re = f_re[:, :, None] * b_re - f_im[:, :, None] * b_im
    bb_im = f_re[:, :, None] * b_im + f_im[:, :, None] * b_re
    eye = jnp.eye(S5_GROUPS, dtype=F32)
    bm_re = jnp.einsum('gpc,gh->gchp', bb_re, eye).reshape(MIX, S5_NSTATE)
    bm_im = jnp.einsum('gpc,gh->gchp', bb_im, eye).reshape(MIX, S5_NSTATE)
    bmat = jnp.concatenate([bm_re, bm_im], axis=1).astype(BF16)
    cm_re = jnp.einsum('gcp,gh->gphc', c_re, eye).reshape(S5_NSTATE, MIX)
    cm_im = jnp.einsum('gcp,gh->gphc', c_im, eye).reshape(S5_NSTATE, MIX)
    cmat = jnp.concatenate([cm_re, -cm_im], axis=0).astype(BF16)
    a_re = jnp.broadcast_to(ab_re.reshape(1, S5_NSTATE), (S5_T, S5_NSTATE))
    a_im = jnp.broadcast_to(ab_im.reshape(1, S5_NSTATE), (S5_T, S5_NSTATE))
    pw_re, pw_im = lax.associative_scan(
        lambda e1, e2: (e1[0] * e2[0] - e1[1] * e2[1], e1[0] * e2[1] + e1[1] * e2[0]),
        (a_re, a_im), axis=0)
    return pl.pallas_call(
        _s5_kernel,
        grid=(bsz, seq // S5_T),
        in_specs=[pl.BlockSpec((1, S5_T, MIX), lambda b, i: (b, i, 0)),
                  _const_spec((MIX, 2 * S5_NSTATE)),
                  _const_spec((S5_T, S5_NSTATE)), _const_spec((S5_T, S5_NSTATE)),
                  _const_spec((2 * S5_NSTATE, MIX)),
                  _const_spec((1, MIX)), _const_spec((MIX, MIX)), _const_spec((1, MIX))],
        out_specs=pl.BlockSpec((1, S5_T, MIX), lambda b, i: (b, i, 0)),
        out_shape=jax.ShapeDtypeStruct((bsz, seq, MIX), F32),
        scratch_shapes=[pltpu.VMEM((2, S5_NSTATE), F32)],
        compiler_params=_params(("parallel", "arbitrary")),
        name="s5",
    )(u, bmat, pw_re, pw_im, cmat, d.reshape(1, MIX), w_glu.astype(BF16), b_glu.reshape(1, MIX))


def _rg_kernel(p_ref, cw_ref, cb_ref, wa_ref, ba_ref, wx_ref, bx_ref, sp_ref,
               o_ref, xbuf_ref, h_ref):
    first = pl.program_id(1) == 0

    @pl.when(first)
    def _():
        xbuf_ref[0:HALO, :] = jnp.zeros((HALO, MIX), F32)
        h_ref[...] = jnp.zeros_like(h_ref)

    p = p_ref[0]
    x = p[:, :MIX]
    y = p[:, MIX:]
    xbuf_ref[HALO:HALO + RG_T, :] = x
    xc = cb_ref[...] + cw_ref[RG_CONV - 1:RG_CONV, :] * x
    for j in range(1, RG_CONV):
        xc = xc + cw_ref[RG_CONV - 1 - j:RG_CONV - j, :] * xbuf_ref[HALO - j:HALO - j + RG_T, :]
    xbuf_ref[0:HALO, :] = x[RG_T - HALO:RG_T, :]
    gate_r = jax.nn.sigmoid(_dot(xc, wa_ref[...]) + ba_ref[...])
    gate_i = jax.nn.sigmoid(_dot(xc, wx_ref[...]) + bx_ref[...])
    log_a = -RG_C * gate_r * sp_ref[...]
    a = jnp.exp(log_a)
    th = jnp.tanh(log_a)
    mult = jnp.sqrt(-2.0 * th / (1.0 - th))
    row = lax.broadcasted_iota(jnp.int32, mult.shape, 0)
    mult = jnp.where(jnp.logical_and(first, row == 0), 1.0, mult)
    b = mult * (gate_i * xc)
    s = 1
    while s < RG_T:
        b = b + a * _shift_rows(b, s, 0.0)
        a = a * _shift_rows(a, s, 1.0)
        s *= 2
    h = b + a * h_ref[...]
    h_ref[...] = h[RG_T - 1:RG_T, :]
    o_ref[0] = h * jax.nn.gelu(y)


def _block_diag(w):
    nh, n, _ = w.shape
    eye = jnp.eye(nh, dtype=w.dtype)
    return jnp.einsum('hij,hg->higj', w, eye).reshape(nh * n, nh * n)


def _rg(p, conv_w, conv_b, w_a, b_a, w_x, b_x, lam):
    bsz, seq, _ = p.shape
    sp = jax.nn.softplus(-lam).reshape(1, MIX)
    row = lambda t: t.reshape(1, MIX)
    return pl.pallas_call(
        _rg_kernel,
        grid=(bsz, seq // RG_T),
        in_specs=[pl.BlockSpec((1, RG_T, 2 * MIX), lambda b, i: (b, i, 0)),
                  _const_spec((RG_CONV, MIX)), _const_spec((1, MIX)),
                  _const_spec((MIX, MIX)), _const_spec((1, MIX)),
                  _const_spec((MIX, MIX)), _const_spec((1, MIX)), _const_spec((1, MIX))],
        out_specs=pl.BlockSpec((1, RG_T, MIX), lambda b, i: (b, i, 0)),
        out_shape=jax.ShapeDtypeStruct((bsz, seq, MIX), F32),
        scratch_shapes=[pltpu.VMEM((HALO + RG_T, MIX), F32), pltpu.VMEM((1, MIX), F32)],
        compiler_params=_params(("parallel", "arbitrary")),
        name="rglru",
    )(p, conv_w, row(conv_b), _block_diag(w_a).astype(BF16), row(b_a),
      _block_diag(w_x).astype(BF16), row(b_x), sp)


GLA_LEVELS = int(math.log2(GLA_T)) + 1


def _gla_consts():
    t = GLA_T
    tri = np.tril(np.ones((t, t), np.float32))
    cs = [tri]
    masks = [np.eye(t, dtype=np.float32)]
    idx = np.arange(t)
    for lvl in range(1, GLA_LEVELS):
        s = 2 ** (lvl - 1)
        mid = (idx // (2 * s)) * (2 * s) + s
        cs.append(tri[mid - 1])
        masks.append(((idx[:, None] ^ idx[None, :]) < 2 * s).astype(np.float32))
    hm_k = np.zeros((GLA_HEADS, GLA_KEY), np.float32)
    hm_v = np.zeros((GLA_HEADS, MIX), np.float32)
    for h in range(GLA_HEADS):
        hm_k[h, h * GLA_DK:(h + 1) * GLA_DK] = 1.0
        hm_v[h, h * GLA_DV:(h + 1) * GLA_DV] = 1.0
    bd = hm_v.T @ hm_k
    seg = (hm_v.T @ hm_v) / GLA_DV
    return (jnp.asarray(np.concatenate(cs, axis=0), BF16), jnp.asarray(np.stack(masks), F32),
            jnp.asarray(hm_k), jnp.asarray(hm_v), jnp.asarray(bd), jnp.asarray(seg, BF16))


def _gla_kernel(p_ref, aup_ref, ab_ref, nw_ref, cs_ref, mask_ref, hmk_ref, hmv_ref, bd_ref,
                seg_ref, o_ref, st_ref):
    @pl.when(pl.program_id(1) == 0)
    def _():
        st_ref[...] = jnp.zeros_like(st_ref)

    t = GLA_T
    p = p_ref[0]
    q = p[:, 0:GLA_KEY] * (GLA_DK ** -0.5)
    k = p[:, GLA_KEY:2 * GLA_KEY]
    v = p[:, 2 * GLA_KEY:2 * GLA_KEY + MIX]
    g = p[:, 2 * GLA_KEY + MIX:2 * GLA_KEY + 2 * MIX]
    lora = p[:, 2 * GLA_KEY + 2 * MIX:]
    z = _dot(lora, aup_ref[...]) + ab_ref[...]
    gk = -_softplus(-z) / GLA_GATE_NORM
    br = _sel_dot(cs_ref[...], gk)
    bc = br[0:t]
    row = lax.broadcasted_iota(jnp.int32, (t, GLA_KEY), 0)
    attn = [jnp.zeros((t, t), F32) for _ in range(GLA_HEADS)]
    for lvl in range(GLA_LEVELS):
        if lvl == 0:
            qt, kt = q, k
        else:
            s = 2 ** (lvl - 1)
            ref = br[lvl * t:(lvl + 1) * t]
            upper = (row & s) != 0
            qt = jnp.where(upper, q * jnp.exp(bc - ref), 0.0)
            kt = jnp.where(upper, 0.0, k * jnp.exp(ref - bc))
        pr = _dot_nt(_stack_heads(qt, hmk_ref, GLA_HEADS), kt)
        m = mask_ref[lvl]
        for h in range(GLA_HEADS):
            attn[h] = attn[h] + pr[h * t:(h + 1) * t] * m
    ov = _dot(jnp.concatenate(attn, axis=0), v)
    o = ov[0:t] * hmv_ref[0:1, :]
    for h in range(1, GLA_HEADS):
        o = o + ov[h * t:(h + 1) * t] * hmv_ref[h:h + 1, :]
    st = st_ref[...]
    o = o + _dot_nt(q * jnp.exp(bc), st)
    b_last = bc[t - 1:t, :]
    kd = k * jnp.exp(b_last - bc)
    st_ref[...] = st * jnp.exp(b_last) + _dot_tn(v, kd) * bd_ref[...]
    ms = _seg_dot(o * o, seg_ref[...])
    o = o * lax.rsqrt(ms + NORM_EPS)
    o_ref[0] = o * nw_ref[...] * (g * jax.nn.sigmoid(g))


def _gla(p, alpha_up, alpha_b, norm_w):
    bsz, seq, _ = p.shape
    aup = jnp.zeros((GLA_WIDTH_PAD - 2 * GLA_KEY - 2 * MIX, GLA_KEY), F32)
    aup = aup.at[:GLA_GATE_LORA].set(alpha_up).astype(BF16)
    cs, masks, hm_k, hm_v, bd, seg = _gla_consts()
    t = GLA_T
    return pl.pallas_call(
        _gla_kernel,
        grid=(bsz, seq // t),
        in_specs=[pl.BlockSpec((1, t, GLA_WIDTH_PAD), lambda b, i: (b, i, 0)),
                  _const_spec(aup.shape), _const_spec((1, GLA_KEY)), _const_spec((1, MIX)),
                  _const_spec(cs.shape), _const_spec(masks.shape), _const_spec(hm_k.shape),
                  _const_spec(hm_v.shape), _const_spec(bd.shape), _const_spec(seg.shape)],
        out_specs=pl.BlockSpec((1, t, MIX), lambda b, i: (b, i, 0)),
        out_shape=jax.ShapeDtypeStruct((bsz, seq, MIX), F32),
        scratch_shapes=[pltpu.VMEM((MIX, GLA_KEY), F32)],
        compiler_params=_params(("parallel", "arbitrary")),
        name="gla",
    )(p, aup, alpha_b.reshape(1, GLA_KEY), norm_w.reshape(1, MIX), cs, masks, hm_k, hm_v, bd, seg)


def _rwkv_consts():
    c, nh, hd = RWKV_C, RWKV_HEADS, RWKV_HEAD
    hm = np.zeros((nh, MIX), np.float32)
    for h in range(nh):
        hm[h, h * hd:(h + 1) * hd] = 1.0
    tri = np.tril(np.ones((c, c), np.float32))
    eye_h = np.eye(nh, dtype=np.float32)
    incl = np.kron(eye_h, tri)
    strict = np.kron(eye_h, tri - np.eye(c, dtype=np.float32))
    ident = np.eye(nh * c, dtype=np.float32)
    seg = hm.T @ hm
    return (jnp.asarray(hm), jnp.asarray(tri, BF16), jnp.asarray(strict), jnp.asarray(incl),
            jnp.asarray(ident), jnp.asarray(seg, BF16), jnp.asarray(seg))


def _rwkv_kernel(p_ref, mu_ref, w0_ref, wup_ref, a0_ref, aup_ref, gup_ref, kk_ref, ka_ref,
                 rk_ref, lnw_ref, lnb_ref, hm_ref, tri_ref, strict_ref, incl_ref, ident_ref,
                 seg_ref, bd_ref, o_ref, prev_ref, mt_ref):
    @pl.when(pl.program_id(1) == 0)
    def _():
        prev_ref[...] = jnp.zeros_like(prev_ref)
        mt_ref[...] = jnp.zeros_like(mt_ref)

    c, nh = RWKV_C, RWKV_HEADS
    p0 = p_ref[0]
    row = lax.broadcasted_iota(jnp.int32, p0.shape, 0)
    shifted = jnp.where(row == 0, prev_ref[...], pltpu.roll(p0, 1, axis=0))
    prev_ref[...] = p0[c - 1:c, :]
    p = p0 + (shifted - p0) * mu_ref[...]
    r = p[:, 0:MIX]
    k = p[:, MIX:2 * MIX]
    v = p[:, 2 * MIX:3 * MIX]
    lora = p[:, 3 * MIX:]
    w = -_softplus(-(w0_ref[...] + _dot(jnp.tanh(lora), wup_ref[...]))) - 0.5
    wlog = -jnp.exp(w)
    a = jax.nn.sigmoid(a0_ref[...] + _dot(lora, aup_ref[...]))
    g = _dot(jax.nn.sigmoid(lora), gup_ref[...])
    kk = k * kk_ref[...]
    kk = kk / jnp.maximum(jnp.sqrt(_seg_dot(kk * kk, seg_ref[...])), 1e-12)
    k = k * (1.0 + (a - 1.0) * ka_ref[...])
    bvec = kk * a
    cl = _sel_dot(tri_ref[...], wlog)
    g_in = jnp.exp(cl)
    g_ex = jnp.exp(cl - wlog)
    g_inv = jnp.exp(-cl)
    at = -kk * g_ex
    rt = r * g_in
    x = jnp.concatenate([_stack_heads(at, hm_ref, nh), _stack_heads(rt, hm_ref, nh)], axis=0)
    y = jnp.concatenate([_stack_heads(bvec * g_inv, hm_ref, nh),
                         _stack_heads(k * g_inv, hm_ref, nh)], axis=0)
    pr = _dot_nt(x, y)
    n = nh * c
    strict = strict_ref[...]
    incl = incl_ref[...]
    nab = pr[0:n, 0:n] * strict
    aak = pr[0:n, n:2 * n] * strict
    arb = pr[n:2 * n, 0:n] * incl
    ark = pr[n:2 * n, n:2 * n] * incl
    tm = ident_ref[...] + nab
    npow = nab
    for _ in range(int(math.log2(c)) - 1):
        npow = _dot(npow, npow)
        tm = tm + _dot(tm, npow)
    mt = mt_ref[...]
    am = _dot_nt(jnp.concatenate([at, rt], axis=0), mt)
    vbd = _stack_heads(v, hm_ref, nh)
    rhs = _dot(aak, vbd) + _stack_heads(am[0:c], hm_ref, nh)
    u = _dot(tm, rhs)
    o_st = (_stack_heads(am[c:2 * c], hm_ref, nh)
            + _dot(jnp.concatenate([arb, ark], axis=1), jnp.concatenate([u, vbd], axis=0)))
    o = _unstack_heads(o_st, nh)
    uc = _unstack_heads(u, nh)
    cl_last = cl[c - 1:c, :]
    tail = jnp.exp(cl_last - cl)
    upd = _dot_tn(jnp.concatenate([uc, v], axis=0),
                  jnp.concatenate([bvec * tail, k * tail], axis=0))
    mt_ref[...] = mt * jnp.exp(cl_last) + upd * bd_ref[...]
    inv_n = 1.0 / RWKV_HEAD
    mean = _seg_dot(o, seg_ref[...]) * inv_n
    oc = o - mean
    var = _seg_dot(oc * oc, seg_ref[...]) * inv_n
    on = oc * lax.rsqrt(var + RWKV_GN_EPS) * lnw_ref[...] + lnb_ref[...]
    bonus = _seg_dot(r * k * rk_ref[...], seg_ref[...]) * v
    o_ref[0] = (on + bonus) * g


def _rwkv(p, mu, w0, w_up, a0, a_up, g_up, k_k, k_a, r_k, ln_w, ln_b):
    bsz, seq, _ = p.shape
    c = RWKV_C
    row = lambda t: t.reshape(1, -1)
    lora_w = jnp.zeros((3, 128, MIX), F32)
    lora_w = lora_w.at[0, 0:32].set(w_up).at[1, 32:64].set(a_up).at[2, 64:128].set(g_up).astype(BF16)
    hm, tri, strict, incl, ident, seg, bd = _rwkv_consts()
    consts = [row(mu), row(w0), lora_w[0], row(a0), lora_w[1], lora_w[2], row(k_k), row(k_a),
              row(r_k), row(ln_w), row(ln_b), hm, tri, strict, incl, ident, seg, bd]
    return pl.pallas_call(
        _rwkv_kernel,
        grid=(bsz, seq // c),
        in_specs=[pl.BlockSpec((1, c, RWKV_WIDTH), lambda b, i: (b, i, 0))]
        + [_const_spec(t.shape) for t in consts],
        out_specs=pl.BlockSpec((1, c, MIX), lambda b, i: (b, i, 0)),
        out_shape=jax.ShapeDtypeStruct((bsz, seq, MIX), F32),
        scratch_shapes=[pltpu.VMEM((1, RWKV_WIDTH), F32), pltpu.VMEM((MIX, MIX), F32)],
        compiler_params=_params(("parallel", "arbitrary")),
        name="rwkv7",
    )(p, *consts)


def _merge_kernel(x_ref, ya_ref, yb_ref, yc_ref, yd_ref, npre_ref, wg_ref, wb_ref, wo_ref,
                  npost_ref, o_ref):
    x = x_ref[...]
    h = _rms(x, npre_ref[...]).astype(BF16)
    merged = None
    for kbr, y_ref in enumerate((ya_ref, yb_ref, yc_ref, yd_ref)):
        gate = jax.nn.sigmoid(jnp.dot(h, wg_ref[kbr], preferred_element_type=F32))
        term = gate * _dot(y_ref[...], wb_ref[kbr])
        merged = term if merged is None else merged + term
    m = _dot(merged, wo_ref[...])
    o_ref[...] = x + _rms(m, npost_ref[...])


def _merge(x2, ys, npre, wg, wb, wo, npost):
    n_tok = x2.shape[0]
    t = MERGE_T
    return pl.pallas_call(
        _merge_kernel,
        grid=(n_tok // t,),
        in_specs=[pl.BlockSpec((t, D_MODEL), lambda i: (i, 0))]
        + [pl.BlockSpec((t, MIX), lambda i: (i, 0))] * 4
        + [_const_spec((1, D_MODEL)), _const_spec(wg.shape), _const_spec(wb.shape),
           _const_spec(wo.shape), _const_spec((1, D_MODEL))],
        out_specs=pl.BlockSpec((t, D_MODEL), lambda i: (i, 0)),
        out_shape=jax.ShapeDtypeStruct((n_tok, D_MODEL), F32),
        compiler_params=_params(("parallel",)),
        name="merge",
    )(x2, *ys, npre, wg, wb, wo, npost)


def _ffn_kernel(x_ref, npre_ref, win_ref, cw_ref, cb_ref, wout_ref, npost_ref, o_ref, gbuf_ref,
                *, tiles_per_seq):
    @pl.when(pl.program_id(0) % tiles_per_seq == 0)
    def _():
        gbuf_ref[0:HALO, :] = jnp.zeros((HALO, FFN_DIM), F32)

    t = FFN_T
    x = x_ref[...]
    h = _rms(x, npre_ref[...]).astype(BF16)
    gv = jnp.dot(h, win_ref[...], preferred_element_type=F32)
    gate = gv[:, :FFN_DIM]
    val = gv[:, FFN_DIM:]
    gbuf_ref[HALO:HALO + t, :] = gate
    gc = cb_ref[...] + cw_ref[FFN_CONV - 1:FFN_CONV, :] * gate
    for j in range(1, FFN_CONV):
        gc = gc + cw_ref[FFN_CONV - 1 - j:FFN_CONV - j, :] * gbuf_ref[HALO - j:HALO - j + t, :]
    gbuf_ref[0:HALO, :] = gate[t - HALO:t, :]
    f = _dot(jax.nn.gelu(gc) * val, wout_ref[...])
    o_ref[...] = x + _rms(f, npost_ref[...])


def _ffn(x2, seq, npre, w_in, conv_w, conv_b, w_out, npost):
    n_tok = x2.shape[0]
    t = FFN_T
    return pl.pallas_call(
        functools.partial(_ffn_kernel, tiles_per_seq=seq // t),
        grid=(n_tok // t,),
        in_specs=[pl.BlockSpec((t, D_MODEL), lambda i: (i, 0)), _const_spec((1, D_MODEL)),
                  _const_spec(w_in.shape), _const_spec((FFN_CONV, FFN_DIM)),
                  _const_spec((1, FFN_DIM)), _const_spec(w_out.shape), _const_spec((1, D_MODEL))],
        out_specs=pl.BlockSpec((t, D_MODEL), lambda i: (i, 0)),
        out_shape=jax.ShapeDtypeStruct((n_tok, D_MODEL), F32),
        scratch_shapes=[pltpu.VMEM((HALO + t, FFN_DIM), F32)],
        compiler_params=_params(("arbitrary",)),
        name="convffn",
    )(x2, npre, w_in, conv_w, conv_b, w_out, npost)


def kernel(x, norm_mix_pre, norm_mix_post, norm_ffn_pre, norm_ffn_post, w_in, s5_lambda_re, s5_lambda_im, s5_log_dt, s5_b_re, s5_b_im, s5_c_re, s5_c_im, s5_d, s5_w_glu, s5_b_glu, rwkv_mu, rwkv_w0, rwkv_w_up, rwkv_a0, rwkv_a_up, rwkv_g_up, rwkv_k_k, rwkv_k_a, rwkv_r_k, rwkv_ln_w, rwkv_ln_b, gla_alpha_up, gla_alpha_b, gla_norm_w, rg_conv_w, rg_conv_b, rg_w_a, rg_b_a, rg_w_x, rg_b_x, rg_lambda, w_branch, w_out, ffn_w_in, ffn_conv_w, ffn_conv_b, ffn_w_out):
    bsz, seq, _ = x.shape
    depth = w_in.shape[0]
    n_tok = bsz * seq
    x2 = x.reshape(n_tok, D_MODEL)
    o_rwkv = MIX
    o_gla = o_rwkv + 3 * MIX + 128
    o_rg = o_gla + GLA_WIDTH
    o_gate = o_rg + 2 * MIX
    for l in range(depth):
        w = w_in[l]
        w_s5 = w[:, :o_rwkv].astype(BF16)
        w_rwkv = w[:, o_rwkv:o_gla].astype(BF16)
        w_gla = jnp.pad(w[:, o_gla:o_rg], ((0, 0), (0, GLA_WIDTH_PAD - GLA_WIDTH))).astype(BF16)
        w_rg = w[:, o_rg:o_gate].astype(BF16)
        w_gate = w[:, o_gate:].reshape(D_MODEL, 4, D_MODEL).transpose(1, 0, 2).astype(BF16)
        npre = norm_mix_pre[l].reshape(1, D_MODEL)
        p_s5, p_rwkv, p_gla, p_rg = _proj(x2, npre, [w_s5, w_rwkv, w_gla, w_rg])
        seq3 = lambda t: t.reshape(bsz, seq, t.shape[-1])
        y_a = _s5(seq3(p_s5), s5_lambda_re[l], s5_lambda_im[l], s5_log_dt[l], s5_b_re[l],
                  s5_b_im[l], s5_c_re[l], s5_c_im[l], s5_d[l], s5_w_glu[l], s5_b_glu[l])
        y_b = _rwkv(seq3(p_rwkv), rwkv_mu[l], rwkv_w0[l], rwkv_w_up[l], rwkv_a0[l], rwkv_a_up[l],
                    rwkv_g_up[l], rwkv_k_k[l], rwkv_k_a[l], rwkv_r_k[l], rwkv_ln_w[l],
                    rwkv_ln_b[l])
        y_c = _gla(seq3(p_gla), gla_alpha_up[l], gla_alpha_b[l], gla_norm_w[l])
        y_d = _rg(seq3(p_rg), rg_conv_w[l], rg_conv_b[l], rg_w_a[l], rg_b_a[l], rg_w_x[l],
                  rg_b_x[l], rg_lambda[l])
        ys = [t.reshape(n_tok, MIX) for t in (y_a, y_b, y_c, y_d)]
        x2 = _merge(x2, ys, npre, w_gate, w_branch[l].astype(BF16), w_out[l].astype(BF16),
                    norm_mix_post[l].reshape(1, D_MODEL))
        x2 = _ffn(x2, seq, norm_ffn_pre[l].reshape(1, D_MODEL), ffn_w_in[l].astype(BF16),
                  ffn_conv_w[l], ffn_conv_b[l].reshape(1, FFN_DIM), ffn_w_out[l].astype(BF16),
                  norm_ffn_post[l].reshape(1, D_MODEL))
    return x2.reshape(bsz, seq, D_MODEL)
```

```python
import functools
import math

import numpy as np
import jax
import jax.numpy as jnp
from jax import lax
from jax.experimental import pallas as pl
from jax.experimental.pallas import tpu as pltpu

F32 = jnp.float32
BF16 = jnp.bfloat16

D_MODEL = 1024
MIX = 256
NORM_EPS = 1e-6
S5_GROUPS, S5_GROUP, S5_STATE = 16, 16, 64
S5_NSTATE = S5_GROUPS * S5_STATE
S5_DT_MIN, S5_DT_MAX = 1e-3, 1e-1
RWKV_HEADS, RWKV_HEAD = 4, 64
RWKV_WIDTH = 3 * MIX + 128
RWKV_GN_EPS = 64e-5
GLA_HEADS, GLA_DK, GLA_DV = 4, 32, 64
GLA_KEY = GLA_HEADS * GLA_DK
GLA_GATE_LORA = 16
GLA_GATE_NORM = 16.0
GLA_WIDTH = 2 * GLA_KEY + 2 * MIX + GLA_GATE_LORA
GLA_WIDTH_PAD = 896
RG_CONV = 4
RG_C = 8.0
FFN_DIM = 2816
FFN_CONV = 3
HALO = 8

S5_T = 128
RG_T = 128
GLA_T = 128
RWKV_C = 64
SUB = 8
PROJ_T = 1024
MERGE_T = 512
FFN_T = 512
FFN_GROUPS = 2

VMEM_LIMIT = 56 * 1024 * 1024


def _dot(a, b):
    return jnp.dot(a.astype(BF16), b.astype(BF16), preferred_element_type=F32)


def _dot_nt(a, b):
    return lax.dot_general(a.astype(BF16), b.astype(BF16), (((1,), (1,)), ((), ())),
                           preferred_element_type=F32)


def _dot_tn(a, b):
    return lax.dot_general(a.astype(BF16), b.astype(BF16), (((0,), (0,)), ((), ())),
                           preferred_element_type=F32)


def _split3(x):
    hi = x.astype(BF16)
    r1 = x - hi.astype(F32)
    mid = r1.astype(BF16)
    lo = (r1 - mid.astype(F32)).astype(BF16)
    return hi, mid, lo


def _sel_dot(c, x):
    hi, mid, lo = _split3(x)
    return (jnp.dot(c, hi, preferred_element_type=F32)
            + jnp.dot(c, mid, preferred_element_type=F32)
            + jnp.dot(c, lo, preferred_element_type=F32))


def _seg_dot(x, j):
    hi = x.astype(BF16)
    lo = (x - hi.astype(F32)).astype(BF16)
    m = x.shape[0]
    r = jnp.dot(jnp.concatenate([hi, lo], axis=0), j, preferred_element_type=F32)
    return r[0:m] + r[m:2 * m]


def _rms(x, w):
    return x * lax.rsqrt(jnp.mean(x * x, axis=-1, keepdims=True) + NORM_EPS) * w


def _softplus(y):
    return jnp.maximum(y, 0.0) + jnp.log1p(jnp.exp(-jnp.abs(y)))


def _shift_rows(x, s, fill):
    rolled = pltpu.roll(x, s, axis=0)
    row = lax.broadcasted_iota(jnp.int32, x.shape, 0)
    return jnp.where(row >= s, rolled, fill)


def _stack_heads(x, hm_ref, nheads):
    return jnp.concatenate([x * hm_ref[h:h + 1, :] for h in range(nheads)], axis=0)


def _unstack_heads(x, nheads):
    c = x.shape[0] // nheads
    out = x[0:c]
    for h in range(1, nheads):
        out = out + x[h * c:(h + 1) * c]
    return out


def _round_robin(gens):
    while gens:
        alive = []
        for gen in gens:
            try:
                next(gen)
                alive.append(gen)
            except StopIteration:
                pass
        gens = alive


def _const_spec(shape):
    nd = len(shape)
    return pl.BlockSpec(shape, lambda *_: (0,) * nd, pipeline_mode=pl.Buffered(1))


def _sigmoid(x):
    return 0.5 * jnp.tanh(0.5 * x) + 0.5


def _params(sem):
    return pltpu.CompilerParams(dimension_semantics=sem, vmem_limit_bytes=VMEM_LIMIT)


def _proj_kernel(x_ref, nw_ref, *refs):
    n = len(refs) // 2
    h = _rms(x_ref[...], nw_ref[...]).astype(BF16)
    for w_ref, o_ref in zip(refs[:n], refs[n:]):
        o_ref[...] = jnp.dot(h, w_ref[...], preferred_element_type=F32)


def _proj(x2, nw, ws):
    n_tok = x2.shape[0]
    widths = [w.shape[1] for w in ws]
    return pl.pallas_call(
        _proj_kernel,
        grid=(n_tok // PROJ_T,),
        in_specs=[pl.BlockSpec((PROJ_T, D_MODEL), lambda i: (i, 0)), _const_spec((1, D_MODEL))]
        + [_const_spec((D_MODEL, wd)) for wd in widths],
        out_specs=[pl.BlockSpec((PROJ_T, wd), lambda i: (i, 0)) for wd in widths],
        out_shape=[jax.ShapeDtypeStruct((n_tok, wd), F32) for wd in widths],
        compiler_params=_params(("parallel",)),
        name="proj",
    )(x2, nw, *ws)


def _cmul_add(hre, him, ar, ai, sre, sim):
    return hre + ar * sre - ai * sim, him + ar * sim + ai * sre


def _block_ends(x, nblk):
    return jnp.concatenate([x[j * SUB + SUB - 1:(j + 1) * SUB] for j in range(nblk)], axis=0)


def _s5_kernel(u_ref, bmat_ref, pwre_ref, pwim_ref, locre_ref, locim_ref, blkre_ref, blkim_ref,
               cmat_ref, d_ref, wglu_ref, bglu_ref, o_ref, carry_ref):
    @pl.when(pl.program_id(0) == 0)
    def _():
        carry_ref[...] = jnp.zeros_like(carry_ref)

    t, nblk = S5_T, S5_T // SUB
    for b in range(u_ref.shape[0]):
        u = u_ref[b]
        bu = _dot(u, bmat_ref[...])
        hre = bu[:, :S5_NSTATE]
        him = bu[:, S5_NSTATE:]
        hre = hre.reshape(nblk, SUB, S5_NSTATE)
        him = him.reshape(nblk, SUB, S5_NSTATE)
        for i in range(int(math.log2(SUB))):
            hre, him = _cmul_add(hre, him, locre_ref[i], locim_ref[i],
                                 pltpu.roll(hre, 2 ** i, axis=1), pltpu.roll(him, 2 ** i, axis=1))
        hre = hre.reshape(t, S5_NSTATE)
        him = him.reshape(t, S5_NSTATE)
        fre = _block_ends(hre, nblk)
        fim = _block_ends(him, nblk)
        s = 1
        while s < nblk:
            idx = s * SUB - 1
            fre, fim = _cmul_add(fre, fim, pwre_ref[idx:idx + 1, :], pwim_ref[idx:idx + 1, :],
                                 _shift_rows(fre, s, 0.0), _shift_rows(fim, s, 0.0))
            s *= 2
        cin_re = carry_ref[b, 0:1, :]
        cin_im = carry_ref[b, 1:2, :]
        cre, cim = _cmul_add(_shift_rows(fre, 1, 0.0), _shift_rows(fim, 1, 0.0),
                             blkre_ref[...], blkim_ref[...], cin_re, cin_im)
        nre, nim = _cmul_add(fre[nblk - 1:nblk], fim[nblk - 1:nblk],
                             pwre_ref[t - 1:t, :], pwim_ref[t - 1:t, :], cin_re, cin_im)
        carry_ref[b, 0:1, :] = nre
        carry_ref[b, 1:2, :] = nim
        p_re = pwre_ref[0:SUB, :]
        p_im = pwim_ref[0:SUB, :]
        hs_re, hs_im = [], []
        for j in range(nblk):
            bre, bim = _cmul_add(hre[j * SUB:(j + 1) * SUB], him[j * SUB:(j + 1) * SUB],
                                 p_re, p_im, cre[j:j + 1], cim[j:j + 1])
            hs_re.append(bre)
            hs_im.append(bim)
        h = jnp.concatenate([jnp.concatenate(hs_re, axis=0), jnp.concatenate(hs_im, axis=0)], axis=1)
        y = _dot(h, cmat_ref[...]) + d_ref[...] * u
        y = jax.nn.gelu(y)
        o_ref[b] = y * jax.nn.sigmoid(_dot(y, wglu_ref[...]) + bglu_ref[...])


def _s5(u, lam_re, lam_im, log_dt, b_re, b_im, c_re, c_im, d, w_glu, b_glu):
    bsz, seq, _ = u.shape
    dt = jnp.exp(log_dt)[:, None]
    mag = jnp.exp(lam_re * dt)
    ang = lam_im * dt
    ab_re = mag * jnp.cos(ang)
    ab_im = mag * jnp.sin(ang)
    den = lam_re * lam_re + lam_im * lam_im
    f_re = ((ab_re - 1.0) * lam_re + ab_im * lam_im) / den
    f_im = (ab_im * lam_re - (ab_re - 1.0) * lam_im) / den
    bb_re = f_re[:, :, None] * b_re - f_im[:, :, None] * b_im
    bb_im = f_re[:, :, None] * b_im + f_im[:, :, None] * b_re
    eye = jnp.eye(S5_GROUPS, dtype=F32)
    bm_re = jnp.einsum('gpc,gh->gchp', bb_re, eye).reshape(MIX, S5_NSTATE)
    bm_im = jnp.einsum('gpc,gh->gchp', bb_im, eye).reshape(MIX, S5_NSTATE)
    bmat = jnp.concatenate([bm_re, bm_im], axis=1).astype(BF16)
    cm_re = jnp.einsum('gcp,gh->gphc', c_re, eye).reshape(S5_NSTATE, MIX)
    cm_im = jnp.einsum('gcp,gh->gphc', c_im, eye).reshape(S5_NSTATE, MIX)
    cmat = jnp.concatenate([cm_re, -cm_im], axis=0).astype(BF16)
    a_re = jnp.broadcast_to(ab_re.reshape(1, S5_NSTATE), (S5_T, S5_NSTATE))
    a_im = jnp.broadcast_to(ab_im.reshape(1, S5_NSTATE), (S5_T, S5_NSTATE))
    pw_re, pw_im = lax.associative_scan(
        lambda e1, e2: (e1[0] * e2[0] - e1[1] * e2[1], e1[0] * e2[1] + e1[1] * e2[0]),
        (a_re, a_im), axis=0)
    nlog = int(math.log2(SUB))
    rows = np.arange(SUB)[None, :, None]
    steps = (2 ** np.arange(nlog))[:, None, None]
    live = jnp.asarray(rows >= steps, F32)
    sel = np.array([2 ** i - 1 for i in range(nlog)])
    loc_re = live * pw_re[sel][:, None, :]
    loc_im = live * pw_im[sel][:, None, :]
    nblk = S5_T // SUB
    blk_re = jnp.concatenate([jnp.ones((1, S5_NSTATE), F32), pw_re[SUB - 1::SUB][:nblk - 1]], axis=0)
    blk_im = jnp.concatenate([jnp.zeros((1, S5_NSTATE), F32), pw_im[SUB - 1::SUB][:nblk - 1]], axis=0)
    consts = [bmat, pw_re, pw_im, loc_re, loc_im, blk_re, blk_im, cmat, d.reshape(1, MIX),
              w_glu.astype(BF16), b_glu.reshape(1, MIX)]
    return pl.pallas_call(
        _s5_kernel,
        grid=(seq // S5_T,),
        in_specs=[pl.BlockSpec((bsz, S5_T, MIX), lambda i: (0, i, 0))]
        + [_const_spec(c.shape) for c in consts],
        out_specs=pl.BlockSpec((bsz, S5_T, MIX), lambda i: (0, i, 0)),
        out_shape=jax.ShapeDtypeStruct((bsz, seq, MIX), F32),
        scratch_shapes=[pltpu.VMEM((bsz, 2, S5_NSTATE), F32)],
        compiler_params=_params(("arbitrary",)),
        name="s5",
    )(u, *consts)


def _rg_kernel(p_ref, cw_ref, cb_ref, wa_ref, ba_ref, wx_ref, bx_ref, sp_ref,
               o_ref, xbuf_ref, h_ref):
    first = pl.program_id(0) == 0

    @pl.when(first)
    def _():
        xbuf_ref[:, 0:HALO, :] = jnp.zeros((xbuf_ref.shape[0], HALO, MIX), F32)
        h_ref[...] = jnp.zeros_like(h_ref)

    t, nblk = RG_T, RG_T // SUB
    row = lax.broadcasted_iota(jnp.int32, (t, MIX), 0)
    sub_row = lax.broadcasted_iota(jnp.int32, (nblk, SUB, MIX), 1)
    for bi in range(p_ref.shape[0]):
        p = p_ref[bi]
        x = p[:, :MIX]
        y = p[:, MIX:]
        xbuf_ref[bi, HALO:HALO + t, :] = x
        xc = cb_ref[...] + cw_ref[RG_CONV - 1:RG_CONV, :] * x
        for j in range(1, RG_CONV):
            xc = xc + (cw_ref[RG_CONV - 1 - j:RG_CONV - j, :]
                       * xbuf_ref[bi, HALO - j:HALO - j + t, :])
        xbuf_ref[bi, 0:HALO, :] = x[t - HALO:t, :]
        gate_r = jax.nn.sigmoid(_dot(xc, wa_ref[...]) + ba_ref[...])
        gate_i = jax.nn.sigmoid(_dot(xc, wx_ref[...]) + bx_ref[...])
        log_a = -RG_C * gate_r * sp_ref[...]
        a = jnp.exp(log_a)
        th = jnp.tanh(log_a)
        mult = jnp.sqrt(-2.0 * th / (1.0 - th))
        mult = jnp.where(jnp.logical_and(first, row == 0), 1.0, mult)
        b = mult * (gate_i * xc)
        a = a.reshape(nblk, SUB, MIX)
        b = b.reshape(nblk, SUB, MIX)
        for i in range(int(math.log2(SUB))):
            live = sub_row >= 2 ** i
            b = b + a * jnp.where(live, pltpu.roll(b, 2 ** i, axis=1), 0.0)
            a = a * jnp.where(live, pltpu.roll(a, 2 ** i, axis=1), 1.0)
        a = a.reshape(t, MIX)
        b = b.reshape(t, MIX)
        ea = _block_ends(a, nblk)
        eb = _block_ends(b, nblk)
        s = 1
        while s < nblk:
            eb = eb + ea * _shift_rows(eb, s, 0.0)
            ea = ea * _shift_rows(ea, s, 1.0)
            s *= 2
        h_in = h_ref[bi]
        c = _shift_rows(eb, 1, 0.0) + _shift_rows(ea, 1, 1.0) * h_in
        h_ref[bi] = eb[nblk - 1:nblk] + ea[nblk - 1:nblk] * h_in
        hs = [b[j * SUB:(j + 1) * SUB] + a[j * SUB:(j + 1) * SUB] * c[j:j + 1] for j in range(nblk)]
        o_ref[bi] = jnp.concatenate(hs, axis=0) * jax.nn.gelu(y)


def _block_diag(w):
    nh, n, _ = w.shape
    eye = jnp.eye(nh, dtype=w.dtype)
    return jnp.einsum('hij,hg->higj', w, eye).reshape(nh * n, nh * n)


def _rg(p, conv_w, conv_b, w_a, b_a, w_x, b_x, lam):
    bsz, seq, _ = p.shape
    sp = jax.nn.softplus(-lam).reshape(1, MIX)
    row = lambda t: t.reshape(1, MIX)
    return pl.pallas_call(
        _rg_kernel,
        grid=(seq // RG_T,),
        in_specs=[pl.BlockSpec((bsz, RG_T, 2 * MIX), lambda i: (0, i, 0)),
                  _const_spec((RG_CONV, MIX)), _const_spec((1, MIX)),
                  _const_spec((MIX, MIX)), _const_spec((1, MIX)),
                  _const_spec((MIX, MIX)), _const_spec((1, MIX)), _const_spec((1, MIX))],
        out_specs=pl.BlockSpec((bsz, RG_T, MIX), lambda i: (0, i, 0)),
        out_shape=jax.ShapeDtypeStruct((bsz, seq, MIX), F32),
        scratch_shapes=[pltpu.VMEM((bsz, HALO + RG_T, MIX), F32), pltpu.VMEM((bsz, 1, MIX), F32)],
        compiler_params=_params(("arbitrary",)),
        name="rglru",
    )(p, conv_w, row(conv_b), _block_diag(w_a).astype(BF16), row(b_a),
      _block_diag(w_x).astype(BF16), row(b_x), sp)


GLA_LEVELS = int(math.log2(GLA_T)) + 1


def _gla_consts():
    t = GLA_T
    tri = np.tril(np.ones((t, t), np.float32))
    cs = [tri]
    masks = [np.eye(t, dtype=np.float32)]
    idx = np.arange(t)
    for lvl in range(1, GLA_LEVELS):
        s = 2 ** (lvl - 1)
        mid = (idx // (2 * s)) * (2 * s) + s
        cs.append(tri[mid - 1] - tri)
        masks.append(((idx[:, None] ^ idx[None, :]) < 2 * s).astype(np.float32))
    cs.append(tri[t - 1:t] - tri)
    hm_k = np.zeros((GLA_HEADS, GLA_KEY), np.float32)
    hm_v = np.zeros((GLA_HEADS, MIX), np.float32)
    for h in range(GLA_HEADS):
        hm_k[h, h * GLA_DK:(h + 1) * GLA_DK] = 1.0
        hm_v[h, h * GLA_DV:(h + 1) * GLA_DV] = 1.0
    bd = hm_v.T @ hm_k
    seg = (hm_v.T @ hm_v) / GLA_DV
    return (jnp.asarray(np.concatenate(cs, axis=0), BF16), jnp.asarray(np.stack(masks), F32),
            jnp.asarray(hm_k), jnp.asarray(hm_v), jnp.asarray(bd), jnp.asarray(seg, BF16))


def _gla_kernel(p_ref, aup_ref, ab_ref, nw_ref, cs_ref, mask_ref, hmk_ref, hmv_ref, bd_ref,
                seg_ref, o_ref, st_ref):
    @pl.when(pl.program_id(0) == 0)
    def _():
        st_ref[...] = jnp.zeros_like(st_ref)

    t = GLA_T
    row = lax.broadcasted_iota(jnp.int32, (t, GLA_KEY), 0)
    def sequence(b):
        p = p_ref[b]
        q = p[:, 0:GLA_KEY] * (GLA_DK ** -0.5)
        k = p[:, GLA_KEY:2 * GLA_KEY]
        v = p[:, 2 * GLA_KEY:2 * GLA_KEY + MIX]
        g = p[:, 2 * GLA_KEY + MIX:2 * GLA_KEY + 2 * MIX]
        lora = p[:, 2 * GLA_KEY + 2 * MIX:]
        z = _dot(lora, aup_ref[...]) + ab_ref[...]
        yield
        gk = -_softplus(-z) / GLA_GATE_NORM
        br = _sel_dot(cs_ref[...], gk)
        yield
        bc = br[0:t]
        st = st_ref[b]
        o_inter = _dot_nt(q * jnp.exp(bc), st)
        kd = k * jnp.exp(br[(GLA_LEVELS) * t:(GLA_LEVELS + 1) * t])
        st_ref[b] = st * jnp.exp(bc[t - 1:t, :]) + _dot_tn(v, kd) * bd_ref[...]
        attn = [None] * GLA_HEADS
        for lvl in range(GLA_LEVELS):
            if lvl == 0:
                qt, kt = q, k
            else:
                e = jnp.exp(-jnp.abs(br[lvl * t:(lvl + 1) * t]))
                upper = (row & (2 ** (lvl - 1))) != 0
                qt = jnp.where(upper, q * e, 0.0)
                kt = jnp.where(upper, 0.0, k * e)
            pr = _dot_nt(_stack_heads(qt, hmk_ref, GLA_HEADS), kt)
            yield
            m = mask_ref[lvl]
            for h in range(GLA_HEADS):
                term = pr[h * t:(h + 1) * t] * m
                attn[h] = term if attn[h] is None else attn[h] + term
        ov = _dot(jnp.concatenate(attn, axis=0), v)
        yield
        o = o_inter + ov[0:t] * hmv_ref[0:1, :]
        for h in range(1, GLA_HEADS):
            o = o + ov[h * t:(h + 1) * t] * hmv_ref[h:h + 1, :]
        ms = _seg_dot(o * o, seg_ref[...])
        yield
        o = o * lax.rsqrt(ms + NORM_EPS)
        o_ref[b] = o * nw_ref[...] * (g * jax.nn.sigmoid(g))

    _round_robin([sequence(b) for b in range(p_ref.shape[0])])


def _gla(p, alpha_up, alpha_b, norm_w):
    bsz, seq, _ = p.shape
    aup = jnp.zeros((GLA_WIDTH_PAD - 2 * GLA_KEY - 2 * MIX, GLA_KEY), F32)
    aup = aup.at[:GLA_GATE_LORA].set(alpha_up).astype(BF16)
    cs, masks, hm_k, hm_v, bd, seg = _gla_consts()
    t = GLA_T
    return pl.pallas_call(
        _gla_kernel,
        grid=(seq // t,),
        in_specs=[pl.BlockSpec((bsz, t, GLA_WIDTH_PAD), lambda i: (0, i, 0)),
                  _const_spec(aup.shape), _const_spec((1, GLA_KEY)), _const_spec((1, MIX)),
                  _const_spec(cs.shape), _const_spec(masks.shape), _const_spec(hm_k.shape),
                  _const_spec(hm_v.shape), _const_spec(bd.shape), _const_spec(seg.shape)],
        out_specs=pl.BlockSpec((bsz, t, MIX), lambda i: (0, i, 0)),
        out_shape=jax.ShapeDtypeStruct((bsz, seq, MIX), F32),
        scratch_shapes=[pltpu.VMEM((bsz, MIX, GLA_KEY), F32)],
        compiler_params=_params(("arbitrary",)),
        name="gla",
    )(p, aup, alpha_b.reshape(1, GLA_KEY), norm_w.reshape(1, MIX), cs, masks, hm_k, hm_v, bd, seg)


def _rwkv_consts():
    c, nh, hd = RWKV_C, RWKV_HEADS, RWKV_HEAD
    hm = np.zeros((nh, MIX), np.float32)
    for h in range(nh):
        hm[h, h * hd:(h + 1) * hd] = 1.0
    tri = np.tril(np.ones((c, c), np.float32))
    eye_h = np.eye(nh, dtype=np.float32)
    incl = np.kron(eye_h, tri)
    strict = np.kron(eye_h, tri - np.eye(c, dtype=np.float32))
    ident = np.eye(nh * c, dtype=np.float32)
    seg = hm.T @ hm
    return (jnp.asarray(hm), jnp.asarray(tri, BF16), jnp.asarray(strict), jnp.asarray(incl),
            jnp.asarray(ident), jnp.asarray(seg, BF16), jnp.asarray(seg))


def _rwkv_kernel(p_ref, mu_ref, w0_ref, wup_ref, a0_ref, aup_ref, gup_ref, kk_ref, ka_ref,
                 rk_ref, lnw_ref, lnb_ref, hm_ref, tri_ref, strict_ref, incl_ref, ident_ref,
                 seg_ref, bd_ref, o_ref, prev_ref, mt_ref):
    @pl.when(pl.program_id(0) == 0)
    def _():
        prev_ref[...] = jnp.zeros_like(prev_ref)
        mt_ref[...] = jnp.zeros_like(mt_ref)

    c, nh = RWKV_C, RWKV_HEADS
    n = nh * c
    row = lax.broadcasted_iota(jnp.int32, (c, RWKV_WIDTH), 0)

    def sequence(b):
        p0 = p_ref[b]
        shifted = jnp.where(row == 0, prev_ref[b], pltpu.roll(p0, 1, axis=0))
        prev_ref[b] = p0[c - 1:c, :]
        p = p0 + (shifted - p0) * mu_ref[...]
        r = p[:, 0:MIX]
        k = p[:, MIX:2 * MIX]
        v = p[:, 2 * MIX:3 * MIX]
        lora = p[:, 3 * MIX:]
        w = -_softplus(-(w0_ref[...] + _dot(jnp.tanh(lora), wup_ref[...]))) - 0.5
        wlog = -jnp.exp(w)
        a = jax.nn.sigmoid(a0_ref[...] + _dot(lora, aup_ref[...]))
        g = _dot(jax.nn.sigmoid(lora), gup_ref[...])
        yield
        kk = k * kk_ref[...]
        k = k * (1.0 + (a - 1.0) * ka_ref[...])
        sums = _seg_dot(jnp.concatenate([kk * kk, r * k * rk_ref[...]], axis=0), seg_ref[...])
        cl = _sel_dot(tri_ref[...], wlog)
        yield
        kk = kk / jnp.maximum(jnp.sqrt(sums[0:c]), 1e-12)
        bonus = sums[c:2 * c] * v
        bvec = kk * a
        g_inv = jnp.exp(-cl)
        at = -kk * jnp.exp(cl - wlog)
        rt = r * jnp.exp(cl)
        x = jnp.concatenate([_stack_heads(at, hm_ref, nh), _stack_heads(rt, hm_ref, nh)], axis=0)
        y = jnp.concatenate([_stack_heads(bvec * g_inv, hm_ref, nh),
                             _stack_heads(k * g_inv, hm_ref, nh)], axis=0)
        pr = _dot_nt(x, y)
        mt = mt_ref[b]
        am = _dot_nt(jnp.concatenate([at, rt], axis=0), mt)
        yield
        strict = strict_ref[...]
        incl = incl_ref[...]
        nab = pr[0:n, 0:n] * strict
        aak = pr[0:n, n:2 * n] * strict
        arb = pr[n:2 * n, 0:n] * incl
        ark = pr[n:2 * n, n:2 * n] * incl
        vbd = _stack_heads(v, hm_ref, nh)
        rhs = _dot(aak, vbd) + _stack_heads(am[0:c], hm_ref, nh)
        tm = ident_ref[...] + nab
        npow = nab
        for _ in range(int(math.log2(c)) - 1):
            npow = _dot(npow, npow)
            yield
            tm = tm + _dot(tm, npow)
            yield
        u = _dot(tm, rhs)
        yield
        o_st = (_stack_heads(am[c:2 * c], hm_ref, nh)
                + _dot(jnp.concatenate([arb, ark], axis=1), jnp.concatenate([u, vbd], axis=0)))
        uc = _unstack_heads(u, nh)
        cl_last = cl[c - 1:c, :]
        tail = jnp.exp(cl_last - cl)
        upd = _dot_tn(jnp.concatenate([uc, v], axis=0),
                      jnp.concatenate([bvec * tail, k * tail], axis=0))
        yield
        mt_ref[b] = mt * jnp.exp(cl_last) + upd * bd_ref[...]
        o = _unstack_heads(o_st, nh)
        inv_n = 1.0 / RWKV_HEAD
        mean = _seg_dot(o, seg_ref[...]) * inv_n
        yield
        oc = o - mean
        var = _seg_dot(oc * oc, seg_ref[...]) * inv_n
        yield
        on = oc * lax.rsqrt(var + RWKV_GN_EPS) * lnw_ref[...] + lnb_ref[...]
        o_ref[b] = (on + bonus) * g

    _round_robin([sequence(b) for b in range(p_ref.shape[0])])


def _rwkv(p, mu, w0, w_up, a0, a_up, g_up, k_k, k_a, r_k, ln_w, ln_b):
    bsz, seq, _ = p.shape
    c = RWKV_C
    row = lambda t: t.reshape(1, -1)
    lora_w = jnp.zeros((3, 128, MIX), F32)
    lora_w = lora_w.at[0, 0:32].set(w_up).at[1, 32:64].set(a_up).at[2, 64:128].set(g_up).astype(BF16)
    hm, tri, strict, incl, ident, seg, bd = _rwkv_consts()
    consts = [row(mu), row(w0), lora_w[0], row(a0), lora_w[1], lora_w[2], row(k_k), row(k_a),
              row(r_k), row(ln_w), row(ln_b), hm, tri, strict, incl, ident, seg, bd]
    return pl.pallas_call(
        _rwkv_kernel,
        grid=(seq // c,),
        in_specs=[pl.BlockSpec((bsz, c, RWKV_WIDTH), lambda i: (0, i, 0))]
        + [_const_spec(t.shape) for t in consts],
        out_specs=pl.BlockSpec((bsz, c, MIX), lambda i: (0, i, 0)),
        out_shape=jax.ShapeDtypeStruct((bsz, seq, MIX), F32),
        scratch_shapes=[pltpu.VMEM((bsz, 1, RWKV_WIDTH), F32), pltpu.VMEM((bsz, MIX, MIX), F32)],
        compiler_params=_params(("arbitrary",)),
        name="rwkv7",
    )(p, *consts)


def _merge_kernel(x_ref, ya_ref, yb_ref, yc_ref, yd_ref, npre_ref, wg_ref, wb_ref, wo_ref,
                  npost_ref, o_ref):
    x = x_ref[...]
    h = _rms(x, npre_ref[...]).astype(BF16)
    merged = None
    for kbr, y_ref in enumerate((ya_ref, yb_ref, yc_ref, yd_ref)):
        gate = _sigmoid(jnp.dot(h, wg_ref[kbr], preferred_element_type=F32))
        term = gate * _dot(y_ref[...], wb_ref[kbr])
        merged = term if merged is None else merged + term
    m = _dot(merged, wo_ref[...])
    o_ref[...] = x + _rms(m, npost_ref[...])


def _merge(x2, ys, npre, wg, wb, wo, npost):
    n_tok = x2.shape[0]
    t = MERGE_T
    return pl.pallas_call(
        _merge_kernel,
        grid=(n_tok // t,),
        in_specs=[pl.BlockSpec((t, D_MODEL), lambda i: (i, 0))]
        + [pl.BlockSpec((t, MIX), lambda i: (i, 0))] * 4
        + [_const_spec((1, D_MODEL)), _const_spec(wg.shape), _const_spec(wb.shape),
           _const_spec(wo.shape), _const_spec((1, D_MODEL))],
        out_specs=pl.BlockSpec((t, D_MODEL), lambda i: (i, 0)),
        out_shape=jax.ShapeDtypeStruct((n_tok, D_MODEL), F32),
        compiler_params=_params(("parallel",)),
        name="merge",
    )(x2, *ys, npre, wg, wb, wo, npost)


def _ffn_kernel(x_ref, npre_ref, win_ref, cw_ref, cb_ref, wout_ref, npost_ref, o_ref, gbuf_ref,
                *, tiles_per_seq):
    @pl.when(pl.program_id(0) % tiles_per_seq == 0)
    def _():
        gbuf_ref[0:HALO, :] = jnp.zeros((HALO, FFN_DIM), F32)

    t = FFN_T
    gw = FFN_DIM // FFN_GROUPS
    x = x_ref[...]
    h = _rms(x, npre_ref[...]).astype(BF16)
    f = None
    for grp in range(FFN_GROUPS):
        c0, c1 = grp * gw, (grp + 1) * gw
        gate = jnp.dot(h, win_ref[:, c0:c1], preferred_element_type=F32)
        val = jnp.dot(h, win_ref[:, FFN_DIM + c0:FFN_DIM + c1], preferred_element_type=F32)
        gbuf_ref[HALO:HALO + t, c0:c1] = gate
        gc = cb_ref[:, c0:c1] + cw_ref[FFN_CONV - 1:FFN_CONV, c0:c1] * gate
        for j in range(1, FFN_CONV):
            gc = gc + (cw_ref[FFN_CONV - 1 - j:FFN_CONV - j, c0:c1]
                       * gbuf_ref[HALO - j:HALO - j + t, c0:c1])
        gbuf_ref[0:HALO, c0:c1] = gate[t - HALO:t, :]
        term = _dot(jax.nn.gelu(gc) * val, wout_ref[c0:c1, :])
        f = term if f is None else f + term
    o_ref[...] = x + _rms(f, npost_ref[...])


def _ffn(x2, seq, npre, w_in, conv_w, conv_b, w_out, npost):
    n_tok = x2.shape[0]
    t = FFN_T
    return pl.pallas_call(
        functools.partial(_ffn_kernel, tiles_per_seq=seq // t),
        grid=(n_tok // t,),
        in_specs=[pl.BlockSpec((t, D_MODEL), lambda i: (i, 0)), _const_spec((1, D_MODEL)),
                  _const_spec(w_in.shape), _const_spec((FFN_CONV, FFN_DIM)),
                  _const_spec((1, FFN_DIM)), _const_spec(w_out.shape), _const_spec((1, D_MODEL))],
        out_specs=pl.BlockSpec((t, D_MODEL), lambda i: (i, 0)),
        out_shape=jax.ShapeDtypeStruct((n_tok, D_MODEL), F32),
        scratch_shapes=[pltpu.VMEM((HALO + t, FFN_DIM), F32)],
        compiler_params=_params(("arbitrary",)),
        name="convffn",
    )(x2, npre, w_in, conv_w, conv_b, w_out, npost)


def kernel(x, norm_mix_pre, norm_mix_post, norm_ffn_pre, norm_ffn_post, w_in, s5_lambda_re, s5_lambda_im, s5_log_dt, s5_b_re, s5_b_im, s5_c_re, s5_c_im, s5_d, s5_w_glu, s5_b_glu, rwkv_mu, rwkv_w0, rwkv_w_up, rwkv_a0, rwkv_a_up, rwkv_g_up, rwkv_k_k, rwkv_k_a, rwkv_r_k, rwkv_ln_w, rwkv_ln_b, gla_alpha_up, gla_alpha_b, gla_norm_w, rg_conv_w, rg_conv_b, rg_w_a, rg_b_a, rg_w_x, rg_b_x, rg_lambda, w_branch, w_out, ffn_w_in, ffn_conv_w, ffn_conv_b, ffn_w_out):
    bsz, seq, _ = x.shape
    depth = w_in.shape[0]
    n_tok = bsz * seq
    x2 = x.reshape(n_tok, D_MODEL)
    o_rwkv = MIX
    o_gla = o_rwkv + 3 * MIX + 128
    o_rg = o_gla + GLA_WIDTH
    o_gate = o_rg + 2 * MIX
    for l in range(depth):
        w = w_in[l]
        w_s5 = w[:, :o_rwkv].astype(BF16)
        w_rwkv = w[:, o_rwkv:o_gla].astype(BF16)
        w_gla = jnp.pad(w[:, o_gla:o_rg], ((0, 0), (0, GLA_WIDTH_PAD - GLA_WIDTH))).astype(BF16)
        w_rg = w[:, o_rg:o_gate].astype(BF16)
        w_gate = w[:, o_gate:].reshape(D_MODEL, 4, D_MODEL).transpose(1, 0, 2).astype(BF16)
        npre = norm_mix_pre[l].reshape(1, D_MODEL)
        p_s5, p_rwkv, p_gla, p_rg = _proj(x2, npre, [w_s5, w_rwkv, w_gla, w_rg])
        seq3 = lambda t: t.reshape(bsz, seq, t.shape[-1])
        y_a = _s5(seq3(p_s5), s5_lambda_re[l], s5_lambda_im[l], s5_log_dt[l], s5_b_re[l],
                  s5_b_im[l], s5_c_re[l], s5_c_im[l], s5_d[l], s5_w_glu[l], s5_b_glu[l])
        y_b = _rwkv(seq3(p_rwkv), rwkv_mu[l], rwkv_w0[l], rwkv_w_up[l], rwkv_a0[l], rwkv_a_up[l],
                    rwkv_g_up[l], rwkv_k_k[l], rwkv_k_a[l], rwkv_r_k[l], rwkv_ln_w[l],
                    rwkv_ln_b[l])
        y_c = _gla(seq3(p_gla), gla_alpha_up[l], gla_alpha_b[l], gla_norm_w[l])
        y_d = _rg(seq3(p_rg), rg_conv_w[l], rg_conv_b[l], rg_w_a[l], rg_b_a[l], rg_w_x[l],
                  rg_b_x[l], rg_lambda[l])
        ys = [t.reshape(n_tok, MIX) for t in (y_a, y_b, y_c, y_d)]
        x2 = _merge(x2, ys, npre, w_gate, w_branch[l].astype(BF16), w_out[l].astype(BF16),
                    norm_mix_post[l].reshape(1, D_MODEL))
        x2 = _ffn(x2, seq, norm_ffn_pre[l].reshape(1, D_MODEL), ffn_w_in[l].astype(BF16),
                  ffn_conv_w[l], ffn_conv_b[l].reshape(1, FFN_DIM), ffn_w_out[l].astype(BF16),
                  norm_ffn_post[l].reshape(1, D_MODEL))
    return x2.reshape(bsz, seq, D_MODEL)
```

```python
import functools
import math

import numpy as np
import jax
import jax.numpy as jnp
from jax import lax
from jax.experimental import pallas as pl
from jax.experimental.pallas import tpu as pltpu

F32 = jnp.float32
BF16 = jnp.bfloat16

D_MODEL = 1024
MIX = 256
NORM_EPS = 1e-6
S5_GROUPS, S5_GROUP, S5_STATE = 16, 16, 64
S5_NSTATE = S5_GROUPS * S5_STATE
S5_DT_MIN, S5_DT_MAX = 1e-3, 1e-1
RWKV_HEADS, RWKV_HEAD = 4, 64
RWKV_WIDTH = 3 * MIX + 128
RWKV_GN_EPS = 64e-5
GLA_HEADS, GLA_DK, GLA_DV = 4, 32, 64
GLA_KEY = GLA_HEADS * GLA_DK
GLA_GATE_LORA = 16
GLA_GATE_NORM = 16.0
GLA_WIDTH = 2 * GLA_KEY + 2 * MIX + GLA_GATE_LORA
GLA_WIDTH_PAD = 896
RG_CONV = 4
RG_C = 8.0
FFN_DIM = 2816
FFN_CONV = 3
HALO = 8

S5_T = 128
RG_T = 128
GLA_T = 128
RWKV_C = 64
SUB = 8
PROJ_T = 1024
MERGE_T = 512
FFN_T = 512

VMEM_LIMIT = 56 * 1024 * 1024


def _dot(a, b):
    return jnp.dot(a.astype(BF16), b.astype(BF16), preferred_element_type=F32)


def _dot_nt(a, b):
    return lax.dot_general(a.astype(BF16), b.astype(BF16), (((1,), (1,)), ((), ())),
                           preferred_element_type=F32)


def _dot_tn(a, b):
    return lax.dot_general(a.astype(BF16), b.astype(BF16), (((0,), (0,)), ((), ())),
                           preferred_element_type=F32)


def _split3(x):
    hi = x.astype(BF16)
    r1 = x - hi.astype(F32)
    mid = r1.astype(BF16)
    lo = (r1 - mid.astype(F32)).astype(BF16)
    return hi, mid, lo


def _sel_dot(c, x):
    hi, mid, lo = _split3(x)
    return (jnp.dot(c, hi, preferred_element_type=F32)
            + jnp.dot(c, mid, preferred_element_type=F32)
            + jnp.dot(c, lo, preferred_element_type=F32))


def _seg_dot(x, j):
    hi = x.astype(BF16)
    lo = (x - hi.astype(F32)).astype(BF16)
    m = x.shape[0]
    r = jnp.dot(jnp.concatenate([hi, lo], axis=0), j, preferred_element_type=F32)
    return r[0:m] + r[m:2 * m]


def _seg_dot_left(c, x):
    hi = x.astype(BF16)
    lo = (x - hi.astype(F32)).astype(BF16)
    n = x.shape[1]
    r = jnp.dot(c, jnp.concatenate([hi, lo], axis=1), preferred_element_type=F32)
    return r[:, 0:n] + r[:, n:2 * n]


def _rms(x, w):
    return x * lax.rsqrt(jnp.mean(x * x, axis=-1, keepdims=True) + NORM_EPS) * w


def _softplus(y):
    return jnp.maximum(y, 0.0) + jnp.log1p(jnp.exp(-jnp.abs(y)))


def _shift_rows(x, s, fill):
    rolled = pltpu.roll(x, s, axis=0)
    row = lax.broadcasted_iota(jnp.int32, x.shape, 0)
    return jnp.where(row >= s, rolled, fill)


def _stack_heads(x, hm_ref, nheads):
    xb = x.astype(BF16)
    return jnp.concatenate([xb * hm_ref[h:h + 1, :].astype(BF16) for h in range(nheads)], axis=0)


def _round_robin(gens):
    while gens:
        alive = []
        for gen in gens:
            try:
                next(gen)
                alive.append(gen)
            except StopIteration:
                pass
        gens = alive


def _const_spec(shape):
    nd = len(shape)
    return pl.BlockSpec(shape, lambda *_: (0,) * nd, pipeline_mode=pl.Buffered(1))


def _sigmoid(x):
    return 0.5 * jnp.tanh(0.5 * x) + 0.5


def _params(sem):
    return pltpu.CompilerParams(dimension_semantics=sem, vmem_limit_bytes=VMEM_LIMIT)


def _proj_kernel(x_ref, nw_ref, *refs):
    n = len(refs) // 2
    h = _rms(x_ref[...], nw_ref[...]).astype(BF16)
    for w_ref, o_ref in zip(refs[:n], refs[n:]):
        o_ref[...] = jnp.dot(h, w_ref[...], preferred_element_type=F32)


def _proj(x2, nw, ws):
    n_tok = x2.shape[0]
    widths = [w.shape[1] for w in ws]
    return pl.pallas_call(
        _proj_kernel,
        grid=(n_tok // PROJ_T,),
        in_specs=[pl.BlockSpec((PROJ_T, D_MODEL), lambda i: (i, 0)), _const_spec((1, D_MODEL))]
        + [_const_spec((D_MODEL, wd)) for wd in widths],
        out_specs=[pl.BlockSpec((PROJ_T, wd), lambda i: (i, 0)) for wd in widths],
        out_shape=[jax.ShapeDtypeStruct((n_tok, wd), F32) for wd in widths],
        compiler_params=_params(("parallel",)),
        name="proj",
    )(x2, nw, *ws)


def _cmul_add(hre, him, ar, ai, sre, sim):
    return hre + ar * sre - ai * sim, him + ar * sim + ai * sre


def _block_ends(x, nblk):
    return jnp.concatenate([x[j * SUB + SUB - 1:(j + 1) * SUB] for j in range(nblk)], axis=0)


def _s5_kernel(u_ref, perm_ref, permt_ref, bmat_ref, are_ref, aim_ref, pwre_ref, pwim_ref,
               cmat_ref, d_ref, wglu_ref, bglu_ref, o_ref, h0_ref, hbuf_ref):
    @pl.when(pl.program_id(0) == 0)
    def _():
        h0_ref[...] = jnp.zeros_like(h0_ref)

    bsz, t = u_ref.shape[0], S5_T
    nseg = SUB // bsz
    seglen = t // nseg
    rows = bsz * t
    ns = S5_NSTATE
    u = u_ref[...].reshape(rows, MIX)
    up = jnp.dot(perm_ref[...], u.astype(BF16), preferred_element_type=F32).astype(BF16)
    hbuf_ref[...] = jnp.dot(up, bmat_ref[...], preferred_element_type=F32)
    are = are_ref[...]
    aim = aim_ref[...]
    hre = h0_ref[0]
    him = h0_ref[1]
    for tt in range(seglen):
        r0 = tt * SUB
        hre, him = _cmul_add(hbuf_ref[r0:r0 + SUB, 0:ns], hbuf_ref[r0:r0 + SUB, ns:2 * ns],
                             are, aim, hre, him)
        hbuf_ref[r0:r0 + SUB, 0:ns] = hre
        hbuf_ref[r0:r0 + SUB, ns:2 * ns] = him
    seg = lax.broadcasted_iota(jnp.int32, (SUB, ns), 0) & (nseg - 1)
    later = seg >= 1
    pre = pwre_ref[rows - SUB:rows, :]
    pim = pwim_ref[rows - SUB:rows, :]
    tre, tim = hre, him
    for _ in range(nseg - 1):
        sre = jnp.where(later, pltpu.roll(tre, 1, axis=0), 0.0)
        sim = jnp.where(later, pltpu.roll(tim, 1, axis=0), 0.0)
        tre, tim = _cmul_add(hre, him, pre, pim, sre, sim)
    cre = jnp.where(later, pltpu.roll(tre, 1, axis=0), 0.0)
    cim = jnp.where(later, pltpu.roll(tim, 1, axis=0), 0.0)
    first = seg == 0
    h0_ref[0] = jnp.where(first, pltpu.roll(tre, SUB - (nseg - 1), axis=0), 0.0)
    h0_ref[1] = jnp.where(first, pltpu.roll(tim, SUB - (nseg - 1), axis=0), 0.0)
    for tt in range(seglen):
        r0 = tt * SUB
        fre, fim = _cmul_add(hbuf_ref[r0:r0 + SUB, 0:ns], hbuf_ref[r0:r0 + SUB, ns:2 * ns],
                             pwre_ref[r0:r0 + SUB, :], pwim_ref[r0:r0 + SUB, :], cre, cim)
        hbuf_ref[r0:r0 + SUB, 0:ns] = fre
        hbuf_ref[r0:r0 + SUB, ns:2 * ns] = fim
    yp = _dot(hbuf_ref[...], cmat_ref[...])
    y = _seg_dot_left(permt_ref[...], yp) + d_ref[...] * u
    y = jax.nn.gelu(y)
    y = y * jax.nn.sigmoid(_dot(y, wglu_ref[...]) + bglu_ref[...])
    o_ref[...] = y.reshape(bsz, t, MIX)


def _s5(u, lam_re, lam_im, log_dt, b_re, b_im, c_re, c_im, d, w_glu, b_glu):
    bsz, seq, _ = u.shape
    dt = jnp.exp(log_dt)[:, None]
    mag = jnp.exp(lam_re * dt)
    ang = lam_im * dt
    ab_re = mag * jnp.cos(ang)
    ab_im = mag * jnp.sin(ang)
    den = lam_re * lam_re + lam_im * lam_im
    f_re = ((ab_re - 1.0) * lam_re + ab_im * lam_im) / den
    f_im = (ab_im * lam_re - (ab_re - 1.0) * lam_im) / den
    bb_re = f_re[:, :, None] * b_re - f_im[:, :, None] * b_im
    bb_im = f_re[:, :, None] * b_im + f_im[:, :, None] * b_re
    eye = jnp.eye(S5_GROUPS, dtype=F32)
    bm_re = jnp.einsum('gpc,gh->gchp', bb_re, eye).reshape(MIX, S5_NSTATE)
    bm_im = jnp.einsum('gpc,gh->gchp', bb_im, eye).reshape(MIX, S5_NSTATE)
    bmat = jnp.concatenate([bm_re, bm_im], axis=1).astype(BF16)
    cm_re = jnp.einsum('gcp,gh->gphc', c_re, eye).reshape(S5_NSTATE, MIX)
    cm_im = jnp.einsum('gcp,gh->gphc', c_im, eye).reshape(S5_NSTATE, MIX)
    cmat = jnp.concatenate([cm_re, -cm_im], axis=0).astype(BF16)
    assert SUB % bsz == 0
    seglen = S5_T * bsz // SUB
    pw_re = ab_re.reshape(1, S5_NSTATE)
    pw_im = ab_im.reshape(1, S5_NSTATE)
    while pw_re.shape[0] < seglen:
        top_re, top_im = pw_re[-1:], pw_im[-1:]
        pw_re, pw_im = (jnp.concatenate([pw_re, pw_re * top_re - pw_im * top_im], axis=0),
                        jnp.concatenate([pw_im, pw_re * top_im + pw_im * top_re], axis=0))
    pw_re = jnp.repeat(pw_re, SUB, axis=0)
    pw_im = jnp.repeat(pw_im, SUB, axis=0)
    a8_re, a8_im = pw_re[0:SUB], pw_im[0:SUB]
    rows = bsz * S5_T
    src = (np.arange(rows) % SUB) * seglen + np.arange(rows) // SUB
    perm = np.zeros((rows, rows), np.float32)
    perm[np.arange(rows), src] = 1.0
    consts = [jnp.asarray(perm, BF16), jnp.asarray(perm.T, BF16), bmat, a8_re, a8_im, pw_re, pw_im,
              cmat, d.reshape(1, MIX), w_glu.astype(BF16), b_glu.reshape(1, MIX)]
    return pl.pallas_call(
        _s5_kernel,
        grid=(seq // S5_T,),
        in_specs=[pl.BlockSpec((bsz, S5_T, MIX), lambda i: (0, i, 0))]
        + [_const_spec(c.shape) for c in consts],
        out_specs=pl.BlockSpec((bsz, S5_T, MIX), lambda i: (0, i, 0)),
        out_shape=jax.ShapeDtypeStruct((bsz, seq, MIX), F32),
        scratch_shapes=[pltpu.VMEM((2, SUB, S5_NSTATE), F32),
                        pltpu.VMEM((rows, 2 * S5_NSTATE), F32)],
        compiler_params=_params(("arbitrary",)),
        name="s5",
    )(u, *consts)


def _rg_kernel(p_ref, cw_ref, cb_ref, wa_ref, ba_ref, wx_ref, bx_ref, sp_ref,
               o_ref, xbuf_ref, h_ref):
    first = pl.program_id(0) == 0

    @pl.when(first)
    def _():
        xbuf_ref[:, 0:HALO, :] = jnp.zeros((xbuf_ref.shape[0], HALO, MIX), F32)
        h_ref[...] = jnp.zeros_like(h_ref)

    t, nblk = RG_T, RG_T // SUB
    row = lax.broadcasted_iota(jnp.int32, (t, MIX), 0)
    sub_row = lax.broadcasted_iota(jnp.int32, (nblk, SUB, MIX), 1)
    for bi in range(p_ref.shape[0]):
        p = p_ref[bi]
        x = p[:, :MIX]
        y = p[:, MIX:]
        xbuf_ref[bi, HALO:HALO + t, :] = x
        xc = cb_ref[...] + cw_ref[RG_CONV - 1:RG_CONV, :] * x
        for j in range(1, RG_CONV):
            xc = xc + (cw_ref[RG_CONV - 1 - j:RG_CONV - j, :]
                       * xbuf_ref[bi, HALO - j:HALO - j + t, :])
        xbuf_ref[bi, 0:HALO, :] = x[t - HALO:t, :]
        gate_r = jax.nn.sigmoid(_dot(xc, wa_ref[...]) + ba_ref[...])
        gate_i = jax.nn.sigmoid(_dot(xc, wx_ref[...]) + bx_ref[...])
        log_a = -RG_C * gate_r * sp_ref[...]
        a = jnp.exp(log_a)
        th = jnp.tanh(log_a)
        mult = jnp.sqrt(-2.0 * th / (1.0 - th))
        mult = jnp.where(jnp.logical_and(first, row == 0), 1.0, mult)
        b = mult * (gate_i * xc)
        a = a.reshape(nblk, SUB, MIX)
        b = b.reshape(nblk, SUB, MIX)
        for i in range(int(math.log2(SUB))):
            live = sub_row >= 2 ** i
            b = b + a * jnp.where(live, pltpu.roll(b, 2 ** i, axis=1), 0.0)
            a = a * jnp.where(live, pltpu.roll(a, 2 ** i, axis=1), 1.0)
        a = a.reshape(t, MIX)
        b = b.reshape(t, MIX)
        ea = _block_ends(a, nblk)
        eb = _block_ends(b, nblk)
        s = 1
        while s < nblk:
            eb = eb + ea * _shift_rows(eb, s, 0.0)
            ea = ea * _shift_rows(ea, s, 1.0)
            s *= 2
        h_in = h_ref[bi]
        c = _shift_rows(eb, 1, 0.0) + _shift_rows(ea, 1, 1.0) * h_in
        h_ref[bi] = eb[nblk - 1:nblk] + ea[nblk - 1:nblk] * h_in
        hs = [b[j * SUB:(j + 1) * SUB] + a[j * SUB:(j + 1) * SUB] * c[j:j + 1] for j in range(nblk)]
        o_ref[bi] = jnp.concatenate(hs, axis=0) * jax.nn.gelu(y)


def _block_diag(w):
    nh, n, _ = w.shape
    eye = jnp.eye(nh, dtype=w.dtype)
    return jnp.einsum('hij,hg->higj', w, eye).reshape(nh * n, nh * n)


def _rg(p, conv_w, conv_b, w_a, b_a, w_x, b_x, lam):
    bsz, seq, _ = p.shape
    sp = jax.nn.softplus(-lam).reshape(1, MIX)
    row = lambda t: t.reshape(1, MIX)
    return pl.pallas_call(
        _rg_kernel,
        grid=(seq // RG_T,),
        in_specs=[pl.BlockSpec((bsz, RG_T, 2 * MIX), lambda i: (0, i, 0)),
                  _const_spec((RG_CONV, MIX)), _const_spec((1, MIX)),
                  _const_spec((MIX, MIX)), _const_spec((1, MIX)),
                  _const_spec((MIX, MIX)), _const_spec((1, MIX)), _const_spec((1, MIX))],
        out_specs=pl.BlockSpec((bsz, RG_T, MIX), lambda i: (0, i, 0)),
        out_shape=jax.ShapeDtypeStruct((bsz, seq, MIX), F32),
        scratch_shapes=[pltpu.VMEM((bsz, HALO + RG_T, MIX), F32), pltpu.VMEM((bsz, 1, MIX), F32)],
        compiler_params=_params(("arbitrary",)),
        name="rglru",
    )(p, conv_w, row(conv_b), _block_diag(w_a).astype(BF16), row(b_a),
      _block_diag(w_x).astype(BF16), row(b_x), sp)


GLA_LEVELS = int(math.log2(GLA_T)) + 1


def _gla_consts():
    t = GLA_T
    tri = np.tril(np.ones((t, t), np.float32))
    cs = [tri]
    masks = [np.eye(t, dtype=np.float32)]
    idx = np.arange(t)
    for lvl in range(1, GLA_LEVELS):
        s = 2 ** (lvl - 1)
        mid = (idx // (2 * s)) * (2 * s) + s
        cs.append(tri[mid - 1] - tri)
        masks.append(((idx[:, None] ^ idx[None, :]) < 2 * s).astype(np.float32))
    cs.append(tri[t - 1:t] - tri)
    hm_k = np.zeros((GLA_HEADS, GLA_KEY), np.float32)
    hm_v = np.zeros((GLA_HEADS, MIX), np.float32)
    for h in range(GLA_HEADS):
        hm_k[h, h * GLA_DK:(h + 1) * GLA_DK] = 1.0
        hm_v[h, h * GLA_DV:(h + 1) * GLA_DV] = 1.0
    bd = hm_v.T @ hm_k
    seg = (hm_v.T @ hm_v) / GLA_DV
    return (jnp.asarray(np.concatenate(cs, axis=0), BF16), jnp.asarray(np.stack(masks), F32),
            jnp.asarray(hm_k), jnp.asarray(hm_v), jnp.asarray(bd), jnp.asarray(seg, BF16))


def _gla_kernel(p_ref, aup_ref, ab_ref, nw_ref, cs_ref, mask_ref, hmk_ref, hmv_ref, bd_ref,
                seg_ref, o_ref, st_ref):
    @pl.when(pl.program_id(0) == 0)
    def _():
        st_ref[...] = jnp.zeros_like(st_ref)

    t = GLA_T
    row = lax.broadcasted_iota(jnp.int32, (t, GLA_KEY), 0)
    def sequence(b):
        p = p_ref[b]
        q = p[:, 0:GLA_KEY] * (GLA_DK ** -0.5)
        k = p[:, GLA_KEY:2 * GLA_KEY]
        v = p[:, 2 * GLA_KEY:2 * GLA_KEY + MIX]
        g = p[:, 2 * GLA_KEY + MIX:2 * GLA_KEY + 2 * MIX]
        lora = p[:, 2 * GLA_KEY + 2 * MIX:]
        z = _dot(lora, aup_ref[...]) + ab_ref[...]
        yield
        gk = -_softplus(-z) / GLA_GATE_NORM
        br = _sel_dot(cs_ref[...], gk)
        yield
        bc = br[0:t]
        st = st_ref[b]
        o_inter = _dot_nt(q * jnp.exp(bc), st)
        kd = k * jnp.exp(br[(GLA_LEVELS) * t:(GLA_LEVELS + 1) * t])
        st_ref[b] = st * jnp.exp(bc[t - 1:t, :]) + _dot_tn(v, kd) * bd_ref[...]
        attn = [None] * GLA_HEADS
        for lvl in range(GLA_LEVELS):
            if lvl == 0:
                qt, kt = q, k
            else:
                e = jnp.exp(-jnp.abs(br[lvl * t:(lvl + 1) * t]))
                upper = (row & (2 ** (lvl - 1))) != 0
                qt = jnp.where(upper, q * e, 0.0)
                kt = jnp.where(upper, 0.0, k * e)
            pr = _dot_nt(_stack_heads(qt, hmk_ref, GLA_HEADS), kt)
            yield
            m = mask_ref[lvl]
            for h in range(GLA_HEADS):
                term = pr[h * t:(h + 1) * t] * m
                attn[h] = term if attn[h] is None else attn[h] + term
        ov = _dot(jnp.concatenate(attn, axis=0), v)
        yield
        o = o_inter + ov[0:t] * hmv_ref[0:1, :]
        for h in range(1, GLA_HEADS):
            o = o + ov[h * t:(h + 1) * t] * hmv_ref[h:h + 1, :]
        ms = _seg_dot(o * o, seg_ref[...])
        yield
        o = o * lax.rsqrt(ms + NORM_EPS)
        o_ref[b] = o * nw_ref[...] * (g * jax.nn.sigmoid(g))

    _round_robin([sequence(b) for b in range(p_ref.shape[0])])


def _gla(p, alpha_up, alpha_b, norm_w):
    bsz, seq, _ = p.shape
    aup = jnp.zeros((GLA_WIDTH_PAD - 2 * GLA_KEY - 2 * MIX, GLA_KEY), F32)
    aup = aup.at[:GLA_GATE_LORA].set(alpha_up).astype(BF16)
    cs, masks, hm_k, hm_v, bd, seg = _gla_consts()
    t = GLA_T
    return pl.pallas_call(
        _gla_kernel,
        grid=(seq // t,),
        in_specs=[pl.BlockSpec((bsz, t, GLA_WIDTH_PAD), lambda i: (0, i, 0)),
                  _const_spec(aup.shape), _const_spec((1, GLA_KEY)), _const_spec((1, MIX)),
                  _const_spec(cs.shape), _const_spec(masks.shape), _const_spec(hm_k.shape),
                  _const_spec(hm_v.shape), _const_spec(bd.shape), _const_spec(seg.shape)],
        out_specs=pl.BlockSpec((bsz, t, MIX), lambda i: (0, i, 0)),
        out_shape=jax.ShapeDtypeStruct((bsz, seq, MIX), F32),
        scratch_shapes=[pltpu.VMEM((bsz, MIX, GLA_KEY), F32)],
        compiler_params=_params(("arbitrary",)),
        name="gla",
    )(p, aup, alpha_b.reshape(1, GLA_KEY), norm_w.reshape(1, MIX), cs, masks, hm_k, hm_v, bd, seg)


def _rwkv_consts():
    c, nh, hd = RWKV_C, RWKV_HEADS, RWKV_HEAD
    hm = np.zeros((nh, MIX), np.float32)
    for h in range(nh):
        hm[h, h * hd:(h + 1) * hd] = 1.0
    tri = np.tril(np.ones((c, c), np.float32))
    eye = np.eye(c, dtype=np.float32)
    incl = np.tile(tri, (1, nh))
    strict = np.tile(tri - eye, (1, nh))
    ident = np.tile(eye, (1, nh))
    seg = hm.T @ hm
    return (jnp.asarray(hm), jnp.asarray(tri, BF16), jnp.asarray(strict), jnp.asarray(incl),
            jnp.asarray(ident), jnp.asarray(seg, BF16), jnp.asarray(seg))


def _rwkv_kernel(p_ref, mu_ref, w0_ref, wup_ref, a0_ref, aup_ref, gup_ref, kk_ref, ka_ref,
                 rk_ref, lnw_ref, lnb_ref, hm_ref, tri_ref, strict_ref, incl_ref, ident_ref,
                 seg_ref, bd_ref, o_ref, prev_ref, mt_ref):
    @pl.when(pl.program_id(0) == 0)
    def _():
        prev_ref[...] = jnp.zeros_like(prev_ref)
        mt_ref[...] = jnp.zeros_like(mt_ref)

    c, nh = RWKV_C, RWKV_HEADS
    n = nh * c
    row = lax.broadcasted_iota(jnp.int32, (c, RWKV_WIDTH), 0)

    def sequence(b):
        p0 = p_ref[b]
        shifted = jnp.where(row == 0, prev_ref[b], pltpu.roll(p0, 1, axis=0))
        prev_ref[b] = p0[c - 1:c, :]
        p = p0 + (shifted - p0) * mu_ref[...]
        r = p[:, 0:MIX]
        k = p[:, MIX:2 * MIX]
        v = p[:, 2 * MIX:3 * MIX]
        lora = p[:, 3 * MIX:]
        w = -_softplus(-(w0_ref[...] + _dot(jnp.tanh(lora), wup_ref[...]))) - 0.5
        wlog = -jnp.exp(w)
        a = jax.nn.sigmoid(a0_ref[...] + _dot(lora, aup_ref[...]))
        g = _dot(jax.nn.sigmoid(lora), gup_ref[...])
        yield
        kk = k * kk_ref[...]
        k = k * (1.0 + (a - 1.0) * ka_ref[...])
        sums = _seg_dot(jnp.concatenate([kk * kk, r * k * rk_ref[...]], axis=0), seg_ref[...])
        cl = _sel_dot(tri_ref[...], wlog)
        yield
        kk = kk / jnp.maximum(jnp.sqrt(sums[0:c]), 1e-12)
        bonus = sums[c:2 * c] * v
        bvec = kk * a
        g_inv = jnp.exp(-cl)
        at = -kk * jnp.exp(cl - wlog)
        rt = r * jnp.exp(cl)
        x = jnp.concatenate([at, rt], axis=0)
        y = jnp.concatenate([_stack_heads(bvec * g_inv, hm_ref, nh),
                             _stack_heads(k * g_inv, hm_ref, nh)], axis=0)
        pr = _dot_nt(x, y)
        mt = mt_ref[b]
        am = _dot_nt(x, mt)
        yield
        strict = strict_ref[...]
        incl = incl_ref[...]
        nab = pr[0:c, 0:n] * strict
        aak = pr[0:c, n:2 * n] * strict
        arb = pr[c:2 * c, 0:n] * incl
        ark = pr[c:2 * c, n:2 * n] * incl
        vbd = _stack_heads(v, hm_ref, nh)
        rhs = _dot(aak, vbd) + am[0:c]
        tm = ident_ref[...] + nab
        npow = nab
        nbd = _stack_heads(npow, hm_ref, nh)
        for _ in range(int(math.log2(c)) - 1):
            npow = _dot(npow, nbd)
            yield
            nbd = _stack_heads(npow, hm_ref, nh)
            tm = tm + _dot(tm, nbd)
            yield
        u = _dot(tm, _stack_heads(rhs, hm_ref, nh))
        yield
        o = am[c:2 * c] + _dot(jnp.concatenate([arb, ark], axis=1),
                               jnp.concatenate([_stack_heads(u, hm_ref, nh), vbd], axis=0))
        cl_last = cl[c - 1:c, :]
        tail = jnp.exp(cl_last - cl)
        upd = _dot_tn(jnp.concatenate([u, v], axis=0),
                      jnp.concatenate([bvec * tail, k * tail], axis=0))
        yield
        mt_ref[b] = mt * jnp.exp(cl_last) + upd * bd_ref[...]
        inv_n = 1.0 / RWKV_HEAD
        mean = _seg_dot(o, seg_ref[...]) * inv_n
        yield
        oc = o - mean
        var = _seg_dot(oc * oc, seg_ref[...]) * inv_n
        yield
        on = oc * lax.rsqrt(var + RWKV_GN_EPS) * lnw_ref[...] + lnb_ref[...]
        o_ref[b] = (on + bonus) * g

    _round_robin([sequence(b) for b in range(p_ref.shape[0])])


def _rwkv(p, mu, w0, w_up, a0, a_up, g_up, k_k, k_a, r_k, ln_w, ln_b):
    bsz, seq, _ = p.shape
    c = RWKV_C
    row = lambda t: t.reshape(1, -1)
    lora_w = jnp.zeros((3, 128, MIX), F32)
    lora_w = lora_w.at[0, 0:32].set(w_up).at[1, 32:64].set(a_up).at[2, 64:128].set(g_up).astype(BF16)
    hm, tri, strict, incl, ident, seg, bd = _rwkv_consts()
    consts = [row(mu), row(w0), lora_w[0], row(a0), lora_w[1], lora_w[2], row(k_k), row(k_a),
              row(r_k), row(ln_w), row(ln_b), hm, tri, strict, incl, ident, seg, bd]
    return pl.pallas_call(
        _rwkv_kernel,
        grid=(seq // c,),
        in_specs=[pl.BlockSpec((bsz, c, RWKV_WIDTH), lambda i: (0, i, 0))]
        + [_const_spec(t.shape) for t in consts],
        out_specs=pl.BlockSpec((bsz, c, MIX), lambda i: (0, i, 0)),
        out_shape=jax.ShapeDtypeStruct((bsz, seq, MIX), F32),
        scratch_shapes=[pltpu.VMEM((bsz, 1, RWKV_WIDTH), F32), pltpu.VMEM((bsz, MIX, MIX), F32)],
        compiler_params=_params(("arbitrary",)),
        name="rwkv7",
    )(p, *consts)


def _merge_kernel(x_ref, ya_ref, yb_ref, yc_ref, yd_ref, npre_ref, wg_ref, wb_ref, wo_ref,
                  npost_ref, o_ref):
    x = x_ref[...]
    h = _rms(x, npre_ref[...]).astype(BF16)
    merged = None
    for kbr, y_ref in enumerate((ya_ref, yb_ref, yc_ref, yd_ref)):
        gate = _sigmoid(jnp.dot(h, wg_ref[:, kbr * D_MODEL:(kbr + 1) * D_MODEL],
                                preferred_element_type=F32))
        term = gate * _dot(y_ref[...], wb_ref[kbr])
        merged = term if merged is None else merged + term
    m = _dot(merged, wo_ref[...])
    o_ref[...] = x + _rms(m, npost_ref[...])


def _merge(x2, ys, npre, wg, wb, wo, npost):
    n_tok = x2.shape[0]
    t = MERGE_T
    return pl.pallas_call(
        _merge_kernel,
        grid=(n_tok // t,),
        in_specs=[pl.BlockSpec((t, D_MODEL), lambda i: (i, 0))]
        + [pl.BlockSpec((t, MIX), lambda i: (i, 0))] * 4
        + [_const_spec((1, D_MODEL)), _const_spec(wg.shape), _const_spec(wb.shape),
           _const_spec(wo.shape), _const_spec((1, D_MODEL))],
        out_specs=pl.BlockSpec((t, D_MODEL), lambda i: (i, 0)),
        out_shape=jax.ShapeDtypeStruct((n_tok, D_MODEL), F32),
        compiler_params=_params(("parallel",)),
        name="merge",
    )(x2, *ys, npre, wg, wb, wo, npost)


def _ffn_kernel(x_ref, npre_ref, win_ref, cw_ref, cb_ref, wout_ref, npost_ref, o_ref, gbuf_ref,
                *, tiles_per_seq):
    @pl.when(pl.program_id(0) % tiles_per_seq == 0)
    def _():
        gbuf_ref[0:HALO, :] = jnp.zeros((HALO, FFN_DIM), F32)

    t = FFN_T
    x = x_ref[...]
    h = _rms(x, npre_ref[...]).astype(BF16)
    gate = jnp.dot(h, win_ref[:, :FFN_DIM], preferred_element_type=F32)
    val = jnp.dot(h, win_ref[:, FFN_DIM:], preferred_element_type=F32)
    gbuf_ref[HALO:HALO + t, :] = gate
    gc = cb_ref[...] + cw_ref[FFN_CONV - 1:FFN_CONV, :] * gate
    for j in range(1, FFN_CONV):
        gc = gc + cw_ref[FFN_CONV - 1 - j:FFN_CONV - j, :] * gbuf_ref[HALO - j:HALO - j + t, :]
    gbuf_ref[0:HALO, :] = gate[t - HALO:t, :]
    f = _dot(jax.nn.gelu(gc) * val, wout_ref[...])
    o_ref[...] = x + _rms(f, npost_ref[...])


def _ffn(x2, seq, npre, w_in, conv_w, conv_b, w_out, npost):
    n_tok = x2.shape[0]
    t = FFN_T
    return pl.pallas_call(
        functools.partial(_ffn_kernel, tiles_per_seq=seq // t),
        grid=(n_tok // t,),
        in_specs=[pl.BlockSpec((t, D_MODEL), lambda i: (i, 0)), _const_spec((1, D_MODEL)),
                  _const_spec(w_in.shape), _const_spec((FFN_CONV, FFN_DIM)),
                  _const_spec((1, FFN_DIM)), _const_spec(w_out.shape), _const_spec((1, D_MODEL))],
        out_specs=pl.BlockSpec((t, D_MODEL), lambda i: (i, 0)),
        out_shape=jax.ShapeDtypeStruct((n_tok, D_MODEL), F32),
        scratch_shapes=[pltpu.VMEM((HALO + t, FFN_DIM), F32)],
        compiler_params=_params(("arbitrary",)),
        name="convffn",
    )(x2, npre, w_in, conv_w, conv_b, w_out, npost)


def kernel(x, norm_mix_pre, norm_mix_post, norm_ffn_pre, norm_ffn_post, w_in, s5_lambda_re, s5_lambda_im, s5_log_dt, s5_b_re, s5_b_im, s5_c_re, s5_c_im, s5_d, s5_w_glu, s5_b_glu, rwkv_mu, rwkv_w0, rwkv_w_up, rwkv_a0, rwkv_a_up, rwkv_g_up, rwkv_k_k, rwkv_k_a, rwkv_r_k, rwkv_ln_w, rwkv_ln_b, gla_alpha_up, gla_alpha_b, gla_norm_w, rg_conv_w, rg_conv_b, rg_w_a, rg_b_a, rg_w_x, rg_b_x, rg_lambda, w_branch, w_out, ffn_w_in, ffn_conv_w, ffn_conv_b, ffn_w_out):
    bsz, seq, _ = x.shape
    depth = w_in.shape[0]
    n_tok = bsz * seq
    x2 = x.reshape(n_tok, D_MODEL)
    o_rwkv = MIX
    o_gla = o_rwkv + 3 * MIX + 128
    o_rg = o_gla + GLA_WIDTH
    o_gate = o_rg + 2 * MIX
    for l in range(depth):
        w = w_in[l]
        w_s5 = w[:, :o_rwkv].astype(BF16)
        w_rwkv = w[:, o_rwkv:o_gla].astype(BF16)
        w_gla = jnp.pad(w[:, o_gla:o_rg], ((0, 0), (0, GLA_WIDTH_PAD - GLA_WIDTH))).astype(BF16)
        w_rg = w[:, o_rg:o_gate].astype(BF16)
        w_gate = w[:, o_gate:].astype(BF16)
        npre = norm_mix_pre[l].reshape(1, D_MODEL)
        p_s5, p_rwkv, p_gla, p_rg = _proj(x2, npre, [w_s5, w_rwkv, w_gla, w_rg])
        seq3 = lambda t: t.reshape(bsz, seq, t.shape[-1])
        y_a = _s5(seq3(p_s5), s5_lambda_re[l], s5_lambda_im[l], s5_log_dt[l], s5_b_re[l],
                  s5_b_im[l], s5_c_re[l], s5_c_im[l], s5_d[l], s5_w_glu[l], s5_b_glu[l])
        y_b = _rwkv(seq3(p_rwkv), rwkv_mu[l], rwkv_w0[l], rwkv_w_up[l], rwkv_a0[l], rwkv_a_up[l],
                    rwkv_g_up[l], rwkv_k_k[l], rwkv_k_a[l], rwkv_r_k[l], rwkv_ln_w[l],
                    rwkv_ln_b[l])
        y_c = _gla(seq3(p_gla), gla_alpha_up[l], gla_alpha_b[l], gla_norm_w[l])
        y_d = _rg(seq3(p_rg), rg_conv_w[l], rg_conv_b[l], rg_w_a[l], rg_b_a[l], rg_w_x[l],
                  rg_b_x[l], rg_lambda[l])
        ys = [t.reshape(n_tok, MIX) for t in (y_a, y_b, y_c, y_d)]
        x2 = _merge(x2, ys, npre, w_gate, w_branch[l].astype(BF16), w_out[l].astype(BF16),
                    norm_mix_post[l].reshape(1, D_MODEL))
        x2 = _ffn(x2, seq, norm_ffn_pre[l].reshape(1, D_MODEL), ffn_w_in[l].astype(BF16),
                  ffn_conv_w[l], ffn_conv_b[l].reshape(1, FFN_DIM), ffn_w_out[l].astype(BF16),
                  norm_ffn_post[l].reshape(1, D_MODEL))
    return x2.reshape(bsz, seq, D_MODEL)
```

```python
import functools
import math

import numpy as np
import jax
import jax.numpy as jnp
from jax import lax
from jax.experimental import pallas as pl
from jax.experimental.pallas import tpu as pltpu

F32 = jnp.float32
BF16 = jnp.bfloat16

D_MODEL = 1024
MIX = 256
NORM_EPS = 1e-6
S5_GROUPS, S5_GROUP, S5_STATE = 16, 16, 64
S5_NSTATE = S5_GROUPS * S5_STATE
S5_DT_MIN, S5_DT_MAX = 1e-3, 1e-1
RWKV_HEADS, RWKV_HEAD = 4, 64
RWKV_WIDTH = 3 * MIX + 128
RWKV_GN_EPS = 64e-5
GLA_HEADS, GLA_DK, GLA_DV = 4, 32, 64
GLA_KEY = GLA_HEADS * GLA_DK
GLA_GATE_LORA = 16
GLA_GATE_NORM = 16.0
GLA_WIDTH = 2 * GLA_KEY + 2 * MIX + GLA_GATE_LORA
GLA_WIDTH_PAD = 896
RG_CONV = 4
RG_C = 8.0
FFN_DIM = 2816
FFN_CONV = 3
HALO = 8

MIX_T = 128
S5_T = RG_T = GLA_T = MIX_T
RWKV_C = 64
SUB = 8
PROJ_T = 1024
MERGE_T = 512
FFN_T = 512

VMEM_LIMIT = 56 * 1024 * 1024


def _dot(a, b):
    return jnp.dot(a.astype(BF16), b.astype(BF16), preferred_element_type=F32)


def _dot_nt(a, b):
    return lax.dot_general(a.astype(BF16), b.astype(BF16), (((1,), (1,)), ((), ())),
                           preferred_element_type=F32)


def _dot_tn(a, b):
    return lax.dot_general(a.astype(BF16), b.astype(BF16), (((0,), (0,)), ((), ())),
                           preferred_element_type=F32)


def _split3(x):
    hi = x.astype(BF16)
    r1 = x - hi.astype(F32)
    mid = r1.astype(BF16)
    lo = (r1 - mid.astype(F32)).astype(BF16)
    return hi, mid, lo


def _sel_dot(c, x):
    hi, mid, lo = _split3(x)
    return (jnp.dot(c, hi, preferred_element_type=F32)
            + jnp.dot(c, mid, preferred_element_type=F32)
            + jnp.dot(c, lo, preferred_element_type=F32))


def _seg_dot(x, j):
    hi = x.astype(BF16)
    lo = (x - hi.astype(F32)).astype(BF16)
    m = x.shape[0]
    r = jnp.dot(jnp.concatenate([hi, lo], axis=0), j, preferred_element_type=F32)
    return r[0:m] + r[m:2 * m]


def _seg_dot_left(c, x):
    hi = x.astype(BF16)
    lo = (x - hi.astype(F32)).astype(BF16)
    n = x.shape[1]
    r = jnp.dot(c, jnp.concatenate([hi, lo], axis=1), preferred_element_type=F32)
    return r[:, 0:n] + r[:, n:2 * n]


def _rms(x, w):
    return x * lax.rsqrt(jnp.mean(x * x, axis=-1, keepdims=True) + NORM_EPS) * w


def _softplus(y):
    return jnp.maximum(y, 0.0) + jnp.log1p(jnp.exp(-jnp.abs(y)))


def _shift_rows(x, s, fill):
    rolled = pltpu.roll(x, s, axis=0)
    row = lax.broadcasted_iota(jnp.int32, x.shape, 0)
    return jnp.where(row >= s, rolled, fill)


def _stack_heads(x, hm_ref, nheads):
    xb = x.astype(BF16)
    return jnp.concatenate([xb * hm_ref[h:h + 1, :].astype(BF16) for h in range(nheads)], axis=0)


def _round_robin(gens):
    while gens:
        alive = []
        for gen in gens:
            try:
                next(gen)
                alive.append(gen)
            except StopIteration:
                pass
        gens = alive


def _const_spec(shape):
    nd = len(shape)
    return pl.BlockSpec(shape, lambda *_: (0,) * nd, pipeline_mode=pl.Buffered(1))


def _sigmoid(x):
    return 0.5 * jnp.tanh(0.5 * x) + 0.5


def _params(sem):
    return pltpu.CompilerParams(dimension_semantics=sem, vmem_limit_bytes=VMEM_LIMIT)


def _proj_kernel(x_ref, nw_ref, *refs):
    n = len(refs) // 2
    h = _rms(x_ref[...], nw_ref[...]).astype(BF16)
    for w_ref, o_ref in zip(refs[:n], refs[n:]):
        o_ref[...] = jnp.dot(h, w_ref[...], preferred_element_type=F32)


def _proj(x2, nw, ws):
    n_tok = x2.shape[0]
    widths = [w.shape[1] for w in ws]
    return pl.pallas_call(
        _proj_kernel,
        grid=(n_tok // PROJ_T,),
        in_specs=[pl.BlockSpec((PROJ_T, D_MODEL), lambda i: (i, 0)), _const_spec((1, D_MODEL))]
        + [_const_spec((D_MODEL, wd)) for wd in widths],
        out_specs=[pl.BlockSpec((PROJ_T, wd), lambda i: (i, 0)) for wd in widths],
        out_shape=[jax.ShapeDtypeStruct((n_tok, wd), F32) for wd in widths],
        compiler_params=_params(("parallel",)),
        name="proj",
    )(x2, nw, *ws)


def _cmul_add(hre, him, ar, ai, sre, sim):
    return hre + ar * sre - ai * sim, him + ar * sim + ai * sre


def _block_ends(x, nblk):
    return jnp.concatenate([x[j * SUB + SUB - 1:(j + 1) * SUB] for j in range(nblk)], axis=0)


S5_YIELD_EVERY = 4


def _s5_stream(u_ref, perm_ref, permt_ref, bmat_ref, are_ref, aim_ref, pwre_ref, pwim_ref,
               cmat_ref, d_ref, wglu_ref, bglu_ref, o_ref, h0_ref, hbuf_ref):
    bsz, t = u_ref.shape[0], S5_T
    nseg = SUB // bsz
    seglen = t // nseg
    rows = bsz * t
    ns = S5_NSTATE
    u = u_ref[...].reshape(rows, MIX)
    up = jnp.dot(perm_ref[...], u.astype(BF16), preferred_element_type=F32).astype(BF16)
    hbuf_ref[...] = jnp.dot(up, bmat_ref[...], preferred_element_type=F32)
    yield
    are = are_ref[...]
    aim = aim_ref[...]
    hre = h0_ref[0]
    him = h0_ref[1]
    for tt in range(seglen):
        r0 = tt * SUB
        hre, him = _cmul_add(hbuf_ref[r0:r0 + SUB, 0:ns], hbuf_ref[r0:r0 + SUB, ns:2 * ns],
                             are, aim, hre, him)
        hbuf_ref[r0:r0 + SUB, 0:ns] = hre
        hbuf_ref[r0:r0 + SUB, ns:2 * ns] = him
        if tt % S5_YIELD_EVERY == S5_YIELD_EVERY - 1:
            yield
    seg = lax.broadcasted_iota(jnp.int32, (SUB, ns), 0) & (nseg - 1)
    later = seg >= 1
    pre = pwre_ref[rows - SUB:rows, :]
    pim = pwim_ref[rows - SUB:rows, :]
    tre, tim = hre, him
    for _ in range(nseg - 1):
        sre = jnp.where(later, pltpu.roll(tre, 1, axis=0), 0.0)
        sim = jnp.where(later, pltpu.roll(tim, 1, axis=0), 0.0)
        tre, tim = _cmul_add(hre, him, pre, pim, sre, sim)
    cre = jnp.where(later, pltpu.roll(tre, 1, axis=0), 0.0)
    cim = jnp.where(later, pltpu.roll(tim, 1, axis=0), 0.0)
    first = seg == 0
    h0_ref[0] = jnp.where(first, pltpu.roll(tre, SUB - (nseg - 1), axis=0), 0.0)
    h0_ref[1] = jnp.where(first, pltpu.roll(tim, SUB - (nseg - 1), axis=0), 0.0)
    for tt in range(seglen):
        r0 = tt * SUB
        fre, fim = _cmul_add(hbuf_ref[r0:r0 + SUB, 0:ns], hbuf_ref[r0:r0 + SUB, ns:2 * ns],
                             pwre_ref[r0:r0 + SUB, :], pwim_ref[r0:r0 + SUB, :], cre, cim)
        hbuf_ref[r0:r0 + SUB, 0:ns] = fre
        hbuf_ref[r0:r0 + SUB, ns:2 * ns] = fim
        if tt % S5_YIELD_EVERY == S5_YIELD_EVERY - 1:
            yield
    yp = _dot(hbuf_ref[...], cmat_ref[...])
    yield
    y = _seg_dot_left(permt_ref[...], yp) + d_ref[...] * u
    yield
    y = jax.nn.gelu(y)
    y = y * jax.nn.sigmoid(_dot(y, wglu_ref[...]) + bglu_ref[...])
    o_ref[...] = y.reshape(bsz, t, MIX)


def _s5_consts(bsz, lam_re, lam_im, log_dt, b_re, b_im, c_re, c_im, d, w_glu, b_glu):
    dt = jnp.exp(log_dt)[:, None]
    mag = jnp.exp(lam_re * dt)
    ang = lam_im * dt
    ab_re = mag * jnp.cos(ang)
    ab_im = mag * jnp.sin(ang)
    den = lam_re * lam_re + lam_im * lam_im
    f_re = ((ab_re - 1.0) * lam_re + ab_im * lam_im) / den
    f_im = (ab_im * lam_re - (ab_re - 1.0) * lam_im) / den
    bb_re = f_re[:, :, None] * b_re - f_im[:, :, None] * b_im
    bb_im = f_re[:, :, None] * b_im + f_im[:, :, None] * b_re
    eye = jnp.eye(S5_GROUPS, dtype=F32)
    bm_re = jnp.einsum('gpc,gh->gchp', bb_re, eye).reshape(MIX, S5_NSTATE)
    bm_im = jnp.einsum('gpc,gh->gchp', bb_im, eye).reshape(MIX, S5_NSTATE)
    bmat = jnp.concatenate([bm_re, bm_im], axis=1).astype(BF16)
    cm_re = jnp.einsum('gcp,gh->gphc', c_re, eye).reshape(S5_NSTATE, MIX)
    cm_im = jnp.einsum('gcp,gh->gphc', c_im, eye).reshape(S5_NSTATE, MIX)
    cmat = jnp.concatenate([cm_re, -cm_im], axis=0).astype(BF16)
    assert SUB % bsz == 0
    seglen = S5_T * bsz // SUB
    pw_re = ab_re.reshape(1, S5_NSTATE)
    pw_im = ab_im.reshape(1, S5_NSTATE)
    while pw_re.shape[0] < seglen:
        top_re, top_im = pw_re[-1:], pw_im[-1:]
        pw_re, pw_im = (jnp.concatenate([pw_re, pw_re * top_re - pw_im * top_im], axis=0),
                        jnp.concatenate([pw_im, pw_re * top_im + pw_im * top_re], axis=0))
    pw_re = jnp.repeat(pw_re, SUB, axis=0)
    pw_im = jnp.repeat(pw_im, SUB, axis=0)
    a8_re, a8_im = pw_re[0:SUB], pw_im[0:SUB]
    rows = bsz * S5_T
    src = (np.arange(rows) % SUB) * seglen + np.arange(rows) // SUB
    perm = np.zeros((rows, rows), np.float32)
    perm[np.arange(rows), src] = 1.0
    consts = [jnp.asarray(perm, BF16), jnp.asarray(perm.T, BF16), bmat, a8_re, a8_im, pw_re, pw_im,
              cmat, d.reshape(1, MIX), w_glu.astype(BF16), b_glu.reshape(1, MIX)]
    scratch = [pltpu.VMEM((2, SUB, S5_NSTATE), F32), pltpu.VMEM((rows, 2 * S5_NSTATE), F32)]
    return consts, scratch


def _rg_streams(p_ref, cw_ref, cb_ref, wa_ref, ba_ref, wx_ref, bx_ref, sp_ref,
                o_ref, xbuf_ref, h_ref):
    first = pl.program_id(0) == 0
    t, nblk = RG_T, RG_T // SUB
    row = lax.broadcasted_iota(jnp.int32, (t, MIX), 0)
    sub_row = lax.broadcasted_iota(jnp.int32, (nblk, SUB, MIX), 1)

    def sequence(bi):
        p = p_ref[bi]
        x = p[:, :MIX]
        y = p[:, MIX:]
        xbuf_ref[bi, HALO:HALO + t, :] = x
        xc = cb_ref[...] + cw_ref[RG_CONV - 1:RG_CONV, :] * x
        for j in range(1, RG_CONV):
            xc = xc + (cw_ref[RG_CONV - 1 - j:RG_CONV - j, :]
                       * xbuf_ref[bi, HALO - j:HALO - j + t, :])
        xbuf_ref[bi, 0:HALO, :] = x[t - HALO:t, :]
        za = _dot(xc, wa_ref[...])
        zx = _dot(xc, wx_ref[...])
        yield
        gate_r = jax.nn.sigmoid(za + ba_ref[...])
        gate_i = jax.nn.sigmoid(zx + bx_ref[...])
        log_a = -RG_C * gate_r * sp_ref[...]
        a = jnp.exp(log_a)
        th = jnp.tanh(log_a)
        mult = jnp.sqrt(-2.0 * th / (1.0 - th))
        mult = jnp.where(jnp.logical_and(first, row == 0), 1.0, mult)
        b = mult * (gate_i * xc)
        a = a.reshape(nblk, SUB, MIX)
        b = b.reshape(nblk, SUB, MIX)
        for i in range(int(math.log2(SUB))):
            live = sub_row >= 2 ** i
            b = b + a * jnp.where(live, pltpu.roll(b, 2 ** i, axis=1), 0.0)
            a = a * jnp.where(live, pltpu.roll(a, 2 ** i, axis=1), 1.0)
        a = a.reshape(t, MIX)
        b = b.reshape(t, MIX)
        yield
        ea = _block_ends(a, nblk)
        eb = _block_ends(b, nblk)
        s = 1
        while s < nblk:
            eb = eb + ea * _shift_rows(eb, s, 0.0)
            ea = ea * _shift_rows(ea, s, 1.0)
            s *= 2
        h_in = h_ref[bi]
        c = _shift_rows(eb, 1, 0.0) + _shift_rows(ea, 1, 1.0) * h_in
        h_ref[bi] = eb[nblk - 1:nblk] + ea[nblk - 1:nblk] * h_in
        hs = [b[j * SUB:(j + 1) * SUB] + a[j * SUB:(j + 1) * SUB] * c[j:j + 1] for j in range(nblk)]
        o_ref[bi] = jnp.concatenate(hs, axis=0) * jax.nn.gelu(y)

    return [sequence(bi) for bi in range(p_ref.shape[0])]


def _block_diag(w):
    nh, n, _ = w.shape
    eye = jnp.eye(nh, dtype=w.dtype)
    return jnp.einsum('hij,hg->higj', w, eye).reshape(nh * n, nh * n)


def _rg_consts(bsz, conv_w, conv_b, w_a, b_a, w_x, b_x, lam):
    sp = jax.nn.softplus(-lam).reshape(1, MIX)
    row = lambda t: t.reshape(1, MIX)
    consts = [conv_w, row(conv_b), _block_diag(w_a).astype(BF16), row(b_a),
              _block_diag(w_x).astype(BF16), row(b_x), sp]
    scratch = [pltpu.VMEM((bsz, HALO + RG_T, MIX), F32), pltpu.VMEM((bsz, 1, MIX), F32)]
    return consts, scratch


GLA_LEVELS = int(math.log2(GLA_T)) + 1


def _gla_consts():
    t = GLA_T
    tri = np.tril(np.ones((t, t), np.float32))
    cs = [tri]
    masks = [np.eye(t, dtype=np.float32)]
    idx = np.arange(t)
    for lvl in range(1, GLA_LEVELS):
        s = 2 ** (lvl - 1)
        mid = (idx // (2 * s)) * (2 * s) + s
        cs.append(tri[mid - 1] - tri)
        masks.append(((idx[:, None] ^ idx[None, :]) < 2 * s).astype(np.float32))
    cs.append(tri[t - 1:t] - tri)
    hm_k = np.zeros((GLA_HEADS, GLA_KEY), np.float32)
    hm_v = np.zeros((GLA_HEADS, MIX), np.float32)
    for h in range(GLA_HEADS):
        hm_k[h, h * GLA_DK:(h + 1) * GLA_DK] = 1.0
        hm_v[h, h * GLA_DV:(h + 1) * GLA_DV] = 1.0
    bd = hm_v.T @ hm_k
    seg = (hm_v.T @ hm_v) / GLA_DV
    return (jnp.asarray(np.concatenate(cs, axis=0), BF16), jnp.asarray(np.stack(masks), F32),
            jnp.asarray(hm_k), jnp.asarray(hm_v), jnp.asarray(bd), jnp.asarray(seg, BF16))


def _gla_streams(p_ref, aup_ref, ab_ref, nw_ref, cs_ref, mask_ref, hmk_ref, hmv_ref, bd_ref,
                 seg_ref, o_ref, st_ref):
    t = GLA_T
    row = lax.broadcasted_iota(jnp.int32, (t, GLA_KEY), 0)

    def sequence(b):
        p = p_ref[b]
        q = p[:, 0:GLA_KEY] * (GLA_DK ** -0.5)
        k = p[:, GLA_KEY:2 * GLA_KEY]
        v = p[:, 2 * GLA_KEY:2 * GLA_KEY + MIX]
        g = p[:, 2 * GLA_KEY + MIX:2 * GLA_KEY + 2 * MIX]
        lora = p[:, 2 * GLA_KEY + 2 * MIX:]
        z = _dot(lora, aup_ref[...]) + ab_ref[...]
        yield
        gk = -_softplus(-z) / GLA_GATE_NORM
        br = _sel_dot(cs_ref[...], gk)
        yield
        bc = br[0:t]
        st = st_ref[b]
        o_inter = _dot_nt(q * jnp.exp(bc), st)
        kd = k * jnp.exp(br[(GLA_LEVELS) * t:(GLA_LEVELS + 1) * t])
        st_ref[b] = st * jnp.exp(bc[t - 1:t, :]) + _dot_tn(v, kd) * bd_ref[...]
        attn = [None] * GLA_HEADS
        for lvl in range(GLA_LEVELS):
            if lvl == 0:
                qt, kt = q, k
            else:
                e = jnp.exp(-jnp.abs(br[lvl * t:(lvl + 1) * t]))
                upper = (row & (2 ** (lvl - 1))) != 0
                qt = jnp.where(upper, q * e, 0.0)
                kt = jnp.where(upper, 0.0, k * e)
            pr = _dot_nt(_stack_heads(qt, hmk_ref, GLA_HEADS), kt)
            yield
            m = mask_ref[lvl]
            for h in range(GLA_HEADS):
                term = pr[h * t:(h + 1) * t] * m
                attn[h] = term if attn[h] is None else attn[h] + term
        ov = _dot(jnp.concatenate(attn, axis=0), v)
        yield
        o = o_inter + ov[0:t] * hmv_ref[0:1, :]
        for h in range(1, GLA_HEADS):
            o = o + ov[h * t:(h + 1) * t] * hmv_ref[h:h + 1, :]
        ms = _seg_dot(o * o, seg_ref[...])
        yield
        o = o * lax.rsqrt(ms + NORM_EPS)
        o_ref[b] = o * nw_ref[...] * (g * jax.nn.sigmoid(g))

    return [sequence(b) for b in range(p_ref.shape[0])]


def _gla_params(bsz, alpha_up, alpha_b, norm_w):
    aup = jnp.zeros((GLA_WIDTH_PAD - 2 * GLA_KEY - 2 * MIX, GLA_KEY), F32)
    aup = aup.at[:GLA_GATE_LORA].set(alpha_up).astype(BF16)
    cs, masks, hm_k, hm_v, bd, seg = _gla_consts()
    consts = [aup, alpha_b.reshape(1, GLA_KEY), norm_w.reshape(1, MIX), cs, masks, hm_k, hm_v, bd,
              seg]
    return consts, [pltpu.VMEM((bsz, MIX, GLA_KEY), F32)]


def _rwkv_consts():
    c, nh, hd = RWKV_C, RWKV_HEADS, RWKV_HEAD
    hm = np.zeros((nh, MIX), np.float32)
    for h in range(nh):
        hm[h, h * hd:(h + 1) * hd] = 1.0
    tri = np.tril(np.ones((c, c), np.float32))
    eye = np.eye(c, dtype=np.float32)
    incl = np.tile(tri, (1, nh))
    strict = np.tile(tri - eye, (1, nh))
    ident = np.tile(eye, (1, nh))
    seg = hm.T @ hm
    return (jnp.asarray(hm), jnp.asarray(tri, BF16), jnp.asarray(strict), jnp.asarray(incl),
            jnp.asarray(ident), jnp.asarray(seg, BF16), jnp.asarray(seg))


def _rwkv_streams(p_ref, mu_ref, w0_ref, wup_ref, a0_ref, aup_ref, gup_ref, kk_ref, ka_ref,
                  rk_ref, lnw_ref, lnb_ref, hm_ref, tri_ref, strict_ref, incl_ref, ident_ref,
                  seg_ref, bd_ref, o_ref, prev_ref, mt_ref):
    c, nh = RWKV_C, RWKV_HEADS
    n = nh * c
    row = lax.broadcasted_iota(jnp.int32, (c, RWKV_WIDTH), 0)

    def sequence(b):
        for r0 in range(0, p_ref.shape[1], c):
            yield from chunk(b, r0)

    def chunk(b, r0):
        p0 = p_ref[b, r0:r0 + c, :]
        shifted = jnp.where(row == 0, prev_ref[b], pltpu.roll(p0, 1, axis=0))
        prev_ref[b] = p0[c - 1:c, :]
        p = p0 + (shifted - p0) * mu_ref[...]
        r = p[:, 0:MIX]
        k = p[:, MIX:2 * MIX]
        v = p[:, 2 * MIX:3 * MIX]
        lora = p[:, 3 * MIX:]
        w = -_softplus(-(w0_ref[...] + _dot(jnp.tanh(lora), wup_ref[...]))) - 0.5
        wlog = -jnp.exp(w)
        a = jax.nn.sigmoid(a0_ref[...] + _dot(lora, aup_ref[...]))
        g = _dot(jax.nn.sigmoid(lora), gup_ref[...])
        yield
        kk = k * kk_ref[...]
        k = k * (1.0 + (a - 1.0) * ka_ref[...])
        sums = _seg_dot(jnp.concatenate([kk * kk, r * k * rk_ref[...]], axis=0), seg_ref[...])
        cl = _sel_dot(tri_ref[...], wlog)
        yield
        kk = kk / jnp.maximum(jnp.sqrt(sums[0:c]), 1e-12)
        bonus = sums[c:2 * c] * v
        bvec = kk * a
        g_inv = jnp.exp(-cl)
        at = -kk * jnp.exp(cl - wlog)
        rt = r * jnp.exp(cl)
        x = jnp.concatenate([at, rt], axis=0)
        y = jnp.concatenate([_stack_heads(bvec * g_inv, hm_ref, nh),
                             _stack_heads(k * g_inv, hm_ref, nh)], axis=0)
        pr = _dot_nt(x, y)
        mt = mt_ref[b]
        am = _dot_nt(x, mt)
        yield
        strict = strict_ref[...]
        incl = incl_ref[...]
        nab = pr[0:c, 0:n] * strict
        aak = pr[0:c, n:2 * n] * strict
        arb = pr[c:2 * c, 0:n] * incl
        ark = pr[c:2 * c, n:2 * n] * incl
        vbd = _stack_heads(v, hm_ref, nh)
        rhs = _dot(aak, vbd) + am[0:c]
        tm = ident_ref[...] + nab
        npow = nab
        nbd = _stack_heads(npow, hm_ref, nh)
        for _ in range(int(math.log2(c)) - 1):
            npow = _dot(npow, nbd)
            yield
            nbd = _stack_heads(npow, hm_ref, nh)
            tm = tm + _dot(tm, nbd)
            yield
        u = _dot(tm, _stack_heads(rhs, hm_ref, nh))
        yield
        o = am[c:2 * c] + _dot(jnp.concatenate([arb, ark], axis=1),
                               jnp.concatenate([_stack_heads(u, hm_ref, nh), vbd], axis=0))
        cl_last = cl[c - 1:c, :]
        tail = jnp.exp(cl_last - cl)
        upd = _dot_tn(jnp.concatenate([u, v], axis=0),
                      jnp.concatenate([bvec * tail, k * tail], axis=0))
        yield
        mt_ref[b] = mt * jnp.exp(cl_last) + upd * bd_ref[...]
        inv_n = 1.0 / RWKV_HEAD
        mean = _seg_dot(o, seg_ref[...]) * inv_n
        yield
        oc = o - mean
        var = _seg_dot(oc * oc, seg_ref[...]) * inv_n
        yield
        on = oc * lax.rsqrt(var + RWKV_GN_EPS) * lnw_ref[...] + lnb_ref[...]
        o_ref[b, r0:r0 + c, :] = (on + bonus) * g

    return [sequence(b) for b in range(p_ref.shape[0])]


def _rwkv_params(bsz, mu, w0, w_up, a0, a_up, g_up, k_k, k_a, r_k, ln_w, ln_b):
    row = lambda t: t.reshape(1, -1)
    lora_w = jnp.zeros((3, 128, MIX), F32)
    lora_w = lora_w.at[0, 0:32].set(w_up).at[1, 32:64].set(a_up).at[2, 64:128].set(g_up).astype(BF16)
    hm, tri, strict, incl, ident, seg, bd = _rwkv_consts()
    consts = [row(mu), row(w0), lora_w[0], row(a0), lora_w[1], lora_w[2], row(k_k), row(k_a),
              row(r_k), row(ln_w), row(ln_b), hm, tri, strict, incl, ident, seg, bd]
    scratch = [pltpu.VMEM((bsz, 1, RWKV_WIDTH), F32), pltpu.VMEM((bsz, MIX, MIX), F32)]
    return consts, scratch


def _mixers_kernel(*refs, counts):
    n_s5, n_rwkv, n_gla, n_rg = counts
    it = iter(refs)
    take = lambda k: [next(it) for _ in range(k)]
    u_ref, p_rwkv_ref, p_gla_ref, p_rg_ref = take(4)
    c_s5, c_rwkv, c_gla, c_rg = take(n_s5), take(n_rwkv), take(n_gla), take(n_rg)
    ya_ref, yb_ref, yc_ref, yd_ref = take(4)
    s_s5, s_rwkv, s_gla, s_rg = take(2), take(2), take(1), take(2)
    state_refs = [s_s5[0]] + s_rwkv + s_gla + s_rg

    @pl.when(pl.program_id(0) == 0)
    def _():
        for ref in state_refs:
            ref[...] = jnp.zeros_like(ref)

    rwkv = _rwkv_streams(p_rwkv_ref, *c_rwkv, yb_ref, *s_rwkv)
    gla = _gla_streams(p_gla_ref, *c_gla, yc_ref, *s_gla)
    rg = _rg_streams(p_rg_ref, *c_rg, yd_ref, *s_rg)
    s5 = [_s5_stream(u_ref, *c_s5, ya_ref, *s_s5)]
    gens = s5 + rwkv[:2] + rg[:2] + gla[:2] + rwkv[2:] + rg[2:] + gla[2:]
    _round_robin(gens)


def _mixers(p_s5, p_rwkv, p_gla, p_rg, s5_args, rwkv_args, gla_args, rg_args):
    bsz, seq, _ = p_s5.shape
    t = MIX_T
    c_s5, s_s5 = _s5_consts(bsz, *s5_args)
    c_rwkv, s_rwkv = _rwkv_params(bsz, *rwkv_args)
    c_gla, s_gla = _gla_params(bsz, *gla_args)
    c_rg, s_rg = _rg_consts(bsz, *rg_args)
    consts = c_s5 + c_rwkv + c_gla + c_rg
    acts = [p_s5, p_rwkv, p_gla, p_rg]
    seq_spec = lambda w: pl.BlockSpec((bsz, t, w), lambda i: (0, i, 0))
    return pl.pallas_call(
        functools.partial(_mixers_kernel, counts=(len(c_s5), len(c_rwkv), len(c_gla), len(c_rg))),
        grid=(seq // t,),
        in_specs=[seq_spec(a.shape[-1]) for a in acts] + [_const_spec(c.shape) for c in consts],
        out_specs=[seq_spec(MIX)] * 4,
        out_shape=[jax.ShapeDtypeStruct((bsz, seq, MIX), F32)] * 4,
        scratch_shapes=s_s5 + s_rwkv + s_gla + s_rg,
        compiler_params=_params(("arbitrary",)),
        name="mixers",
    )(*acts, *consts)


def _merge_kernel(x_ref, ya_ref, yb_ref, yc_ref, yd_ref, npre_ref, wg_ref, wb_ref, wo_ref,
                  npost_ref, o_ref):
    x = x_ref[...]
    h = _rms(x, npre_ref[...]).astype(BF16)
    merged = None
    for kbr, y_ref in enumerate((ya_ref, yb_ref, yc_ref, yd_ref)):
        gate = _sigmoid(jnp.dot(h, wg_ref[:, kbr * D_MODEL:(kbr + 1) * D_MODEL],
                                preferred_element_type=F32))
        term = gate * _dot(y_ref[...], wb_ref[kbr])
        merged = term if merged is None else merged + term
    m = _dot(merged, wo_ref[...])
    o_ref[...] = x + _rms(m, npost_ref[...])


def _merge(x2, ys, npre, wg, wb, wo, npost):
    n_tok = x2.shape[0]
    t = MERGE_T
    return pl.pallas_call(
        _merge_kernel,
        grid=(n_tok // t,),
        in_specs=[pl.BlockSpec((t, D_MODEL), lambda i: (i, 0))]
        + [pl.BlockSpec((t, MIX), lambda i: (i, 0))] * 4
        + [_const_spec((1, D_MODEL)), _const_spec(wg.shape), _const_spec(wb.shape),
           _const_spec(wo.shape), _const_spec((1, D_MODEL))],
        out_specs=pl.BlockSpec((t, D_MODEL), lambda i: (i, 0)),
        out_shape=jax.ShapeDtypeStruct((n_tok, D_MODEL), F32),
        compiler_params=_params(("parallel",)),
        name="merge",
    )(x2, *ys, npre, wg, wb, wo, npost)


def _ffn_kernel(x_ref, npre_ref, win_ref, cw_ref, cb_ref, wout_ref, npost_ref, o_ref, gbuf_ref,
                *, tiles_per_seq):
    @pl.when(pl.program_id(0) % tiles_per_seq == 0)
    def _():
        gbuf_ref[0:HALO, :] = jnp.zeros((HALO, FFN_DIM), F32)

    t = FFN_T
    x = x_ref[...]
    h = _rms(x, npre_ref[...]).astype(BF16)
    gate = jnp.dot(h, win_ref[:, :FFN_DIM], preferred_element_type=F32)
    val = jnp.dot(h, win_ref[:, FFN_DIM:], preferred_element_type=F32)
    gbuf_ref[HALO:HALO + t, :] = gate
    gc = cb_ref[...] + cw_ref[FFN_CONV - 1:FFN_CONV, :] * gate
    for j in range(1, FFN_CONV):
        gc = gc + cw_ref[FFN_CONV - 1 - j:FFN_CONV - j, :] * gbuf_ref[HALO - j:HALO - j + t, :]
    gbuf_ref[0:HALO, :] = gate[t - HALO:t, :]
    f = _dot(jax.nn.gelu(gc) * val, wout_ref[...])
    o_ref[...] = x + _rms(f, npost_ref[...])


def _ffn(x2, seq, npre, w_in, conv_w, conv_b, w_out, npost):
    n_tok = x2.shape[0]
    t = FFN_T
    return pl.pallas_call(
        functools.partial(_ffn_kernel, tiles_per_seq=seq // t),
        grid=(n_tok // t,),
        in_specs=[pl.BlockSpec((t, D_MODEL), lambda i: (i, 0)), _const_spec((1, D_MODEL)),
                  _const_spec(w_in.shape), _const_spec((FFN_CONV, FFN_DIM)),
                  _const_spec((1, FFN_DIM)), _const_spec(w_out.shape), _const_spec((1, D_MODEL))],
        out_specs=pl.BlockSpec((t, D_MODEL), lambda i: (i, 0)),
        out_shape=jax.ShapeDtypeStruct((n_tok, D_MODEL), F32),
        scratch_shapes=[pltpu.VMEM((HALO + t, FFN_DIM), F32)],
        compiler_params=_params(("arbitrary",)),
        name="convffn",
    )(x2, npre, w_in, conv_w, conv_b, w_out, npost)


def kernel(x, norm_mix_pre, norm_mix_post, norm_ffn_pre, norm_ffn_post, w_in, s5_lambda_re, s5_lambda_im, s5_log_dt, s5_b_re, s5_b_im, s5_c_re, s5_c_im, s5_d, s5_w_glu, s5_b_glu, rwkv_mu, rwkv_w0, rwkv_w_up, rwkv_a0, rwkv_a_up, rwkv_g_up, rwkv_k_k, rwkv_k_a, rwkv_r_k, rwkv_ln_w, rwkv_ln_b, gla_alpha_up, gla_alpha_b, gla_norm_w, rg_conv_w, rg_conv_b, rg_w_a, rg_b_a, rg_w_x, rg_b_x, rg_lambda, w_branch, w_out, ffn_w_in, ffn_conv_w, ffn_conv_b, ffn_w_out):
    bsz, seq, _ = x.shape
    depth = w_in.shape[0]
    n_tok = bsz * seq
    x2 = x.reshape(n_tok, D_MODEL)
    o_rwkv = MIX
    o_gla = o_rwkv + 3 * MIX + 128
    o_rg = o_gla + GLA_WIDTH
    o_gate = o_rg + 2 * MIX
    for l in range(depth):
        w = w_in[l]
        w_s5 = w[:, :o_rwkv].astype(BF16)
        w_rwkv = w[:, o_rwkv:o_gla].astype(BF16)
        w_gla = jnp.pad(w[:, o_gla:o_rg], ((0, 0), (0, GLA_WIDTH_PAD - GLA_WIDTH))).astype(BF16)
        w_rg = w[:, o_rg:o_gate].astype(BF16)
        w_gate = w[:, o_gate:].astype(BF16)
        npre = norm_mix_pre[l].reshape(1, D_MODEL)
        p_s5, p_rwkv, p_gla, p_rg = _proj(x2, npre, [w_s5, w_rwkv, w_gla, w_rg])
        seq3 = lambda t: t.reshape(bsz, seq, t.shape[-1])
        ys = _mixers(
            seq3(p_s5), seq3(p_rwkv), seq3(p_gla), seq3(p_rg),
            (s5_lambda_re[l], s5_lambda_im[l], s5_log_dt[l], s5_b_re[l], s5_b_im[l], s5_c_re[l],
             s5_c_im[l], s5_d[l], s5_w_glu[l], s5_b_glu[l]),
            (rwkv_mu[l], rwkv_w0[l], rwkv_w_up[l], rwkv_a0[l], rwkv_a_up[l], rwkv_g_up[l],
             rwkv_k_k[l], rwkv_k_a[l], rwkv_r_k[l], rwkv_ln_w[l], rwkv_ln_b[l]),
            (gla_alpha_up[l], gla_alpha_b[l], gla_norm_w[l]),
            (rg_conv_w[l], rg_conv_b[l], rg_w_a[l], rg_b_a[l], rg_w_x[l], rg_b_x[l], rg_lambda[l]))
        ys = [t.reshape(n_tok, MIX) for t in ys]
        x2 = _merge(x2, ys, npre, w_gate, w_branch[l].astype(BF16), w_out[l].astype(BF16),
                    norm_mix_post[l].reshape(1, D_MODEL))
        x2 = _ffn(x2, seq, norm_ffn_pre[l].reshape(1, D_MODEL), ffn_w_in[l].astype(BF16),
                  ffn_conv_w[l], ffn_conv_b[l].reshape(1, FFN_DIM), ffn_w_out[l].astype(BF16),
                  norm_ffn_post[l].reshape(1, D_MODEL))
    return x2.reshape(bsz, seq, D_MODEL)
```

```python
import functools
import math

import numpy as np
import jax
import jax.numpy as jnp
from jax import lax
from jax.experimental import pallas as pl
from jax.experimental.pallas import tpu as pltpu

F32 = jnp.float32
BF16 = jnp.bfloat16

D_MODEL = 1024
MIX = 256
NORM_EPS = 1e-6
S5_GROUPS, S5_GROUP, S5_STATE = 16, 16, 64
S5_NSTATE = S5_GROUPS * S5_STATE
S5_DT_MIN, S5_DT_MAX = 1e-3, 1e-1
RWKV_HEADS, RWKV_HEAD = 4, 64
RWKV_WIDTH = 3 * MIX + 128
RWKV_GN_EPS = 64e-5
GLA_HEADS, GLA_DK, GLA_DV = 4, 32, 64
GLA_KEY = GLA_HEADS * GLA_DK
GLA_GATE_LORA = 16
GLA_GATE_NORM = 16.0
GLA_WIDTH = 2 * GLA_KEY + 2 * MIX + GLA_GATE_LORA
GLA_WIDTH_PAD = 896
RG_CONV = 4
RG_C = 8.0
FFN_DIM = 2816
FFN_CONV = 3
HALO = 8

MIX_T = 128
S5_T = RG_T = GLA_T = MIX_T
RWKV_C = 64
SUB = 8
PROJ_T = 1024
MERGE_T = 512
FFN_T = 512

VMEM_LIMIT = 56 * 1024 * 1024


def _dot(a, b):
    return jnp.dot(a.astype(BF16), b.astype(BF16), preferred_element_type=F32)


def _dot_nt(a, b):
    return lax.dot_general(a.astype(BF16), b.astype(BF16), (((1,), (1,)), ((), ())),
                           preferred_element_type=F32)


def _dot_tn(a, b):
    return lax.dot_general(a.astype(BF16), b.astype(BF16), (((0,), (0,)), ((), ())),
                           preferred_element_type=F32)


def _split3(x):
    hi = x.astype(BF16)
    r1 = x - hi.astype(F32)
    mid = r1.astype(BF16)
    lo = (r1 - mid.astype(F32)).astype(BF16)
    return hi, mid, lo


def _sel_dot(c, x):
    hi, mid, lo = _split3(x)
    return (jnp.dot(c, hi, preferred_element_type=F32)
            + jnp.dot(c, mid, preferred_element_type=F32)
            + jnp.dot(c, lo, preferred_element_type=F32))


def _seg_dot(x, j):
    hi = x.astype(BF16)
    lo = (x - hi.astype(F32)).astype(BF16)
    m = x.shape[0]
    r = jnp.dot(jnp.concatenate([hi, lo], axis=0), j, preferred_element_type=F32)
    return r[0:m] + r[m:2 * m]


def _seg_dot_left(c, x):
    hi = x.astype(BF16)
    lo = (x - hi.astype(F32)).astype(BF16)
    n = x.shape[1]
    r = jnp.dot(c, jnp.concatenate([hi, lo], axis=1), preferred_element_type=F32)
    return r[:, 0:n] + r[:, n:2 * n]


def _rms(x, w):
    return x * lax.rsqrt(jnp.mean(x * x, axis=-1, keepdims=True) + NORM_EPS) * w


def _softplus(y):
    return jnp.maximum(y, 0.0) + jnp.log1p(jnp.exp(-jnp.abs(y)))


def _shift_rows(x, s, fill):
    rolled = pltpu.roll(x, s, axis=0)
    row = lax.broadcasted_iota(jnp.int32, x.shape, 0)
    return jnp.where(row >= s, rolled, fill)


def _stack_heads(x, hm_ref, nheads):
    xb = x.astype(BF16)
    return jnp.concatenate([xb * hm_ref[h:h + 1, :].astype(BF16) for h in range(nheads)], axis=0)


def _round_robin(gens):
    while gens:
        alive = []
        for gen in gens:
            try:
                next(gen)
                alive.append(gen)
            except StopIteration:
                pass
        gens = alive


def _const_spec(shape):
    nd = len(shape)
    return pl.BlockSpec(shape, lambda *_: (0,) * nd, pipeline_mode=pl.Buffered(1))


def _layer_spec(arr, l):
    nd = arr.ndim
    return pl.BlockSpec((None,) + arr.shape[1:], lambda *_: (l,) + (0,) * (nd - 1),
                        pipeline_mode=pl.Buffered(1))


def _sigmoid(x):
    return 0.5 * jnp.tanh(0.5 * x) + 0.5


def _params(sem):
    return pltpu.CompilerParams(dimension_semantics=sem, vmem_limit_bytes=VMEM_LIMIT)


def _proj_kernel(x_ref, nw_ref, *refs):
    n = len(refs) // 2
    h = _rms(x_ref[...], nw_ref[...]).astype(BF16)
    for w_ref, o_ref in zip(refs[:n], refs[n:]):
        o_ref[...] = jnp.dot(h, w_ref[...], preferred_element_type=F32)


def _proj(l, x2, nw, ws):
    n_tok = x2.shape[0]
    widths = [w.shape[-1] for w in ws]
    return pl.pallas_call(
        _proj_kernel,
        grid=(n_tok // PROJ_T,),
        in_specs=[pl.BlockSpec((PROJ_T, D_MODEL), lambda i: (i, 0)), _layer_spec(nw, l)]
        + [_layer_spec(w, l) for w in ws],
        out_specs=[pl.BlockSpec((PROJ_T, wd), lambda i: (i, 0)) for wd in widths],
        out_shape=[jax.ShapeDtypeStruct((n_tok, wd), F32) for wd in widths],
        compiler_params=_params(("parallel",)),
        name="proj",
    )(x2, nw, *ws)


def _cmul_add(hre, him, ar, ai, sre, sim):
    return hre + ar * sre - ai * sim, him + ar * sim + ai * sre


def _block_ends(x, nblk):
    return jnp.concatenate([x[j * SUB + SUB - 1:(j + 1) * SUB] for j in range(nblk)], axis=0)


S5_YIELD_EVERY = 4


def _s5_stream(u_ref, bmat_ref, are_ref, aim_ref, pwre_ref, pwim_ref, cmat_ref, d_ref, wglu_ref,
               bglu_ref, perm_ref, permt_ref, o_ref, h0_ref, hbuf_ref):
    bsz, t = u_ref.shape[0], S5_T
    nseg = SUB // bsz
    seglen = t // nseg
    rows = bsz * t
    ns = S5_NSTATE
    u = u_ref[...].reshape(rows, MIX)
    up = jnp.dot(perm_ref[...], u.astype(BF16), preferred_element_type=F32).astype(BF16)
    hbuf_ref[...] = jnp.dot(up, bmat_ref[...], preferred_element_type=F32)
    yield
    are = are_ref[...]
    aim = aim_ref[...]
    hre = h0_ref[0]
    him = h0_ref[1]
    for tt in range(seglen):
        r0 = tt * SUB
        hre, him = _cmul_add(hbuf_ref[r0:r0 + SUB, 0:ns], hbuf_ref[r0:r0 + SUB, ns:2 * ns],
                             are, aim, hre, him)
        hbuf_ref[r0:r0 + SUB, 0:ns] = hre
        hbuf_ref[r0:r0 + SUB, ns:2 * ns] = him
        if tt % S5_YIELD_EVERY == S5_YIELD_EVERY - 1:
            yield
    seg = lax.broadcasted_iota(jnp.int32, (SUB, ns), 0) & (nseg - 1)
    later = seg >= 1
    pre = pwre_ref[rows - SUB:rows, :]
    pim = pwim_ref[rows - SUB:rows, :]
    tre, tim = hre, him
    for _ in range(nseg - 1):
        sre = jnp.where(later, pltpu.roll(tre, 1, axis=0), 0.0)
        sim = jnp.where(later, pltpu.roll(tim, 1, axis=0), 0.0)
        tre, tim = _cmul_add(hre, him, pre, pim, sre, sim)
    cre = jnp.where(later, pltpu.roll(tre, 1, axis=0), 0.0)
    cim = jnp.where(later, pltpu.roll(tim, 1, axis=0), 0.0)
    first = seg == 0
    h0_ref[0] = jnp.where(first, pltpu.roll(tre, SUB - (nseg - 1), axis=0), 0.0)
    h0_ref[1] = jnp.where(first, pltpu.roll(tim, SUB - (nseg - 1), axis=0), 0.0)
    for tt in range(seglen):
        r0 = tt * SUB
        fre, fim = _cmul_add(hbuf_ref[r0:r0 + SUB, 0:ns], hbuf_ref[r0:r0 + SUB, ns:2 * ns],
                             pwre_ref[r0:r0 + SUB, :], pwim_ref[r0:r0 + SUB, :], cre, cim)
        hbuf_ref[r0:r0 + SUB, 0:ns] = fre
        hbuf_ref[r0:r0 + SUB, ns:2 * ns] = fim
        if tt % S5_YIELD_EVERY == S5_YIELD_EVERY - 1:
            yield
    yp = _dot(hbuf_ref[...], cmat_ref[...])
    yield
    y = _seg_dot_left(permt_ref[...], yp) + d_ref[...] * u
    yield
    y = jax.nn.gelu(y)
    y = y * jax.nn.sigmoid(_dot(y, wglu_ref[...]) + bglu_ref[...])
    o_ref[...] = y.reshape(bsz, t, MIX)


def _s5_shared(bsz):
    rows = bsz * S5_T
    seglen = rows // SUB
    src = (np.arange(rows) % SUB) * seglen + np.arange(rows) // SUB
    perm = np.zeros((rows, rows), np.float32)
    perm[np.arange(rows), src] = 1.0
    return [jnp.asarray(perm, BF16), jnp.asarray(perm.T, BF16)]


def _s5_layer_consts(bsz, lam_re, lam_im, log_dt, b_re, b_im, c_re, c_im, d, w_glu, b_glu):
    dt = jnp.exp(log_dt)[:, None]
    mag = jnp.exp(lam_re * dt)
    ang = lam_im * dt
    ab_re = mag * jnp.cos(ang)
    ab_im = mag * jnp.sin(ang)
    den = lam_re * lam_re + lam_im * lam_im
    f_re = ((ab_re - 1.0) * lam_re + ab_im * lam_im) / den
    f_im = (ab_im * lam_re - (ab_re - 1.0) * lam_im) / den
    bb_re = f_re[:, :, None] * b_re - f_im[:, :, None] * b_im
    bb_im = f_re[:, :, None] * b_im + f_im[:, :, None] * b_re
    eye = jnp.eye(S5_GROUPS, dtype=F32)
    bm_re = jnp.einsum('gpc,gh->gchp', bb_re, eye).reshape(MIX, S5_NSTATE)
    bm_im = jnp.einsum('gpc,gh->gchp', bb_im, eye).reshape(MIX, S5_NSTATE)
    bmat = jnp.concatenate([bm_re, bm_im], axis=1).astype(BF16)
    cm_re = jnp.einsum('gcp,gh->gphc', c_re, eye).reshape(S5_NSTATE, MIX)
    cm_im = jnp.einsum('gcp,gh->gphc', c_im, eye).reshape(S5_NSTATE, MIX)
    cmat = jnp.concatenate([cm_re, -cm_im], axis=0).astype(BF16)
    assert SUB % bsz == 0
    seglen = S5_T * bsz // SUB
    pw_re = ab_re.reshape(1, S5_NSTATE)
    pw_im = ab_im.reshape(1, S5_NSTATE)
    while pw_re.shape[0] < seglen:
        top_re, top_im = pw_re[-1:], pw_im[-1:]
        pw_re, pw_im = (jnp.concatenate([pw_re, pw_re * top_re - pw_im * top_im], axis=0),
                        jnp.concatenate([pw_im, pw_re * top_im + pw_im * top_re], axis=0))
    pw_re = jnp.repeat(pw_re, SUB, axis=0)
    pw_im = jnp.repeat(pw_im, SUB, axis=0)
    a8_re, a8_im = pw_re[0:SUB], pw_im[0:SUB]
    return (bmat, a8_re, a8_im, pw_re, pw_im, cmat, d.reshape(1, MIX), w_glu.astype(BF16),
            b_glu.reshape(1, MIX))


def _rg_streams(p_ref, cw_ref, cb_ref, wa_ref, ba_ref, wx_ref, bx_ref, sp_ref,
                o_ref, xbuf_ref, h_ref):
    first = pl.program_id(0) == 0
    t, nblk = RG_T, RG_T // SUB
    row = lax.broadcasted_iota(jnp.int32, (t, MIX), 0)
    sub_row = lax.broadcasted_iota(jnp.int32, (nblk, SUB, MIX), 1)

    def sequence(bi):
        p = p_ref[bi]
        x = p[:, :MIX]
        y = p[:, MIX:]
        xbuf_ref[bi, HALO:HALO + t, :] = x
        xc = cb_ref[...] + cw_ref[RG_CONV - 1:RG_CONV, :] * x
        for j in range(1, RG_CONV):
            xc = xc + (cw_ref[RG_CONV - 1 - j:RG_CONV - j, :]
                       * xbuf_ref[bi, HALO - j:HALO - j + t, :])
        xbuf_ref[bi, 0:HALO, :] = x[t - HALO:t, :]
        za = _dot(xc, wa_ref[...])
        zx = _dot(xc, wx_ref[...])
        yield
        gate_r = jax.nn.sigmoid(za + ba_ref[...])
        gate_i = jax.nn.sigmoid(zx + bx_ref[...])
        log_a = -RG_C * gate_r * sp_ref[...]
        a = jnp.exp(log_a)
        th = jnp.tanh(log_a)
        mult = jnp.sqrt(-2.0 * th / (1.0 - th))
        mult = jnp.where(jnp.logical_and(first, row == 0), 1.0, mult)
        b = mult * (gate_i * xc)
        a = a.reshape(nblk, SUB, MIX)
        b = b.reshape(nblk, SUB, MIX)
        for i in range(int(math.log2(SUB))):
            live = sub_row >= 2 ** i
            b = b + a * jnp.where(live, pltpu.roll(b, 2 ** i, axis=1), 0.0)
            a = a * jnp.where(live, pltpu.roll(a, 2 ** i, axis=1), 1.0)
        a = a.reshape(t, MIX)
        b = b.reshape(t, MIX)
        yield
        ea = _block_ends(a, nblk)
        eb = _block_ends(b, nblk)
        s = 1
        while s < nblk:
            eb = eb + ea * _shift_rows(eb, s, 0.0)
            ea = ea * _shift_rows(ea, s, 1.0)
            s *= 2
        h_in = h_ref[bi]
        c = _shift_rows(eb, 1, 0.0) + _shift_rows(ea, 1, 1.0) * h_in
        h_ref[bi] = eb[nblk - 1:nblk] + ea[nblk - 1:nblk] * h_in
        hs = [b[j * SUB:(j + 1) * SUB] + a[j * SUB:(j + 1) * SUB] * c[j:j + 1] for j in range(nblk)]
        o_ref[bi] = jnp.concatenate(hs, axis=0) * jax.nn.gelu(y)

    return [sequence(bi) for bi in range(p_ref.shape[0])]


def _block_diag(w):
    nh, n, _ = w.shape
    eye = jnp.eye(nh, dtype=w.dtype)
    return jnp.einsum('hij,hg->higj', w, eye).reshape(nh * n, nh * n)


def _rg_layer_consts(conv_w, conv_b, w_a, b_a, w_x, b_x, lam):
    sp = jax.nn.softplus(-lam).reshape(1, MIX)
    row = lambda t: t.reshape(1, MIX)
    return (conv_w, row(conv_b), _block_diag(w_a).astype(BF16), row(b_a),
            _block_diag(w_x).astype(BF16), row(b_x), sp)


GLA_LEVELS = int(math.log2(GLA_T)) + 1


def _gla_consts():
    t = GLA_T
    tri = np.tril(np.ones((t, t), np.float32))
    cs = [tri]
    masks = [np.eye(t, dtype=np.float32)]
    idx = np.arange(t)
    for lvl in range(1, GLA_LEVELS):
        s = 2 ** (lvl - 1)
        mid = (idx // (2 * s)) * (2 * s) + s
        cs.append(tri[mid - 1] - tri)
        masks.append(((idx[:, None] ^ idx[None, :]) < 2 * s).astype(np.float32))
    cs.append(tri[t - 1:t] - tri)
    hm_k = np.zeros((GLA_HEADS, GLA_KEY), np.float32)
    hm_v = np.zeros((GLA_HEADS, MIX), np.float32)
    for h in range(GLA_HEADS):
        hm_k[h, h * GLA_DK:(h + 1) * GLA_DK] = 1.0
        hm_v[h, h * GLA_DV:(h + 1) * GLA_DV] = 1.0
    bd = hm_v.T @ hm_k
    seg = (hm_v.T @ hm_v) / GLA_DV
    return (jnp.asarray(np.concatenate(cs, axis=0), BF16), jnp.asarray(np.stack(masks), F32),
            jnp.asarray(hm_k), jnp.asarray(hm_v), jnp.asarray(bd), jnp.asarray(seg, BF16))


def _gla_streams(p_ref, aup_ref, ab_ref, nw_ref, cs_ref, mask_ref, hmk_ref, hmv_ref, bd_ref,
                 seg_ref, o_ref, st_ref):
    t = GLA_T
    row = lax.broadcasted_iota(jnp.int32, (t, GLA_KEY), 0)

    def sequence(b):
        p = p_ref[b]
        q = p[:, 0:GLA_KEY] * (GLA_DK ** -0.5)
        k = p[:, GLA_KEY:2 * GLA_KEY]
        v = p[:, 2 * GLA_KEY:2 * GLA_KEY + MIX]
        g = p[:, 2 * GLA_KEY + MIX:2 * GLA_KEY + 2 * MIX]
        lora = p[:, 2 * GLA_KEY + 2 * MIX:]
        z = _dot(lora, aup_ref[...]) + ab_ref[...]
        yield
        gk = -_softplus(-z) / GLA_GATE_NORM
        br = _sel_dot(cs_ref[...], gk)
        yield
        bc = br[0:t]
        st = st_ref[b]
        o_inter = _dot_nt(q * jnp.exp(bc), st)
        kd = k * jnp.exp(br[(GLA_LEVELS) * t:(GLA_LEVELS + 1) * t])
        st_ref[b] = st * jnp.exp(bc[t - 1:t, :]) + _dot_tn(v, kd) * bd_ref[...]
        attn = [None] * GLA_HEADS
        for lvl in range(GLA_LEVELS):
            if lvl == 0:
                qt, kt = q, k
            else:
                e = jnp.exp(-jnp.abs(br[lvl * t:(lvl + 1) * t]))
                upper = (row & (2 ** (lvl - 1))) != 0
                qt = jnp.where(upper, q * e, 0.0)
                kt = jnp.where(upper, 0.0, k * e)
            pr = _dot_nt(_stack_heads(qt, hmk_ref, GLA_HEADS), kt)
            yield
            m = mask_ref[lvl]
            for h in range(GLA_HEADS):
                term = pr[h * t:(h + 1) * t] * m
                attn[h] = term if attn[h] is None else attn[h] + term
        ov = _dot(jnp.concatenate(attn, axis=0), v)
        yield
        o = o_inter + ov[0:t] * hmv_ref[0:1, :]
        for h in range(1, GLA_HEADS):
            o = o + ov[h * t:(h + 1) * t] * hmv_ref[h:h + 1, :]
        ms = _seg_dot(o * o, seg_ref[...])
        yield
        o = o * lax.rsqrt(ms + NORM_EPS)
        o_ref[b] = o * nw_ref[...] * (g * jax.nn.sigmoid(g))

    return [sequence(b) for b in range(p_ref.shape[0])]


def _gla_layer_consts(alpha_up, alpha_b, norm_w):
    aup = jnp.zeros((GLA_WIDTH_PAD - 2 * GLA_KEY - 2 * MIX, GLA_KEY), F32)
    aup = aup.at[:GLA_GATE_LORA].set(alpha_up).astype(BF16)
    return aup, alpha_b.reshape(1, GLA_KEY), norm_w.reshape(1, MIX)


def _rwkv_consts():
    c, nh, hd = RWKV_C, RWKV_HEADS, RWKV_HEAD
    hm = np.zeros((nh, MIX), np.float32)
    for h in range(nh):
        hm[h, h * hd:(h + 1) * hd] = 1.0
    tri = np.tril(np.ones((c, c), np.float32))
    eye = np.eye(c, dtype=np.float32)
    incl = np.tile(tri, (1, nh))
    strict = np.tile(tri - eye, (1, nh))
    ident = np.tile(eye, (1, nh))
    seg = hm.T @ hm
    return (jnp.asarray(hm), jnp.asarray(tri, BF16), jnp.asarray(strict), jnp.asarray(incl),
            jnp.asarray(ident), jnp.asarray(seg, BF16), jnp.asarray(seg))


def _rwkv_streams(p_ref, mu_ref, w0_ref, wup_ref, a0_ref, aup_ref, gup_ref, kk_ref, ka_ref,
                  rk_ref, lnw_ref, lnb_ref, hm_ref, tri_ref, strict_ref, incl_ref, ident_ref,
                  seg_ref, bd_ref, o_ref, prev_ref, mt_ref):
    c, nh = RWKV_C, RWKV_HEADS
    n = nh * c
    row = lax.broadcasted_iota(jnp.int32, (c, RWKV_WIDTH), 0)

    def sequence(b):
        for r0 in range(0, p_ref.shape[1], c):
            yield from chunk(b, r0)

    def chunk(b, r0):
        p0 = p_ref[b, r0:r0 + c, :]
        shifted = jnp.where(row == 0, prev_ref[b], pltpu.roll(p0, 1, axis=0))
        prev_ref[b] = p0[c - 1:c, :]
        p = p0 + (shifted - p0) * mu_ref[...]
        r = p[:, 0:MIX]
        k = p[:, MIX:2 * MIX]
        v = p[:, 2 * MIX:3 * MIX]
        lora = p[:, 3 * MIX:]
        w = -_softplus(-(w0_ref[...] + _dot(jnp.tanh(lora), wup_ref[...]))) - 0.5
        wlog = -jnp.exp(w)
        a = jax.nn.sigmoid(a0_ref[...] + _dot(lora, aup_ref[...]))
        g = _dot(jax.nn.sigmoid(lora), gup_ref[...])
        yield
        kk = k * kk_ref[...]
        k = k * (1.0 + (a - 1.0) * ka_ref[...])
        sums = _seg_dot(jnp.concatenate([kk * kk, r * k * rk_ref[...]], axis=0), seg_ref[...])
        cl = _sel_dot(tri_ref[...], wlog)
        yield
        kk = kk / jnp.maximum(jnp.sqrt(sums[0:c]), 1e-12)
        bonus = sums[c:2 * c] * v
        bvec = kk * a
        g_inv = jnp.exp(-cl)
        at = -kk * jnp.exp(cl - wlog)
        rt = r * jnp.exp(cl)
        x = jnp.concatenate([at, rt], axis=0)
        y = jnp.concatenate([_stack_heads(bvec * g_inv, hm_ref, nh),
                             _stack_heads(k * g_inv, hm_ref, nh)], axis=0)
        pr = _dot_nt(x, y)
        mt = mt_ref[b]
        am = _dot_nt(x, mt)
        yield
        strict = strict_ref[...]
        incl = incl_ref[...]
        nab = pr[0:c, 0:n] * strict
        aak = pr[0:c, n:2 * n] * strict
        arb = pr[c:2 * c, 0:n] * incl
        ark = pr[c:2 * c, n:2 * n] * incl
        vbd = _stack_heads(v, hm_ref, nh)
        rhs = _dot(aak, vbd) + am[0:c]
        tm = ident_ref[...] + nab
        npow = nab
        nbd = _stack_heads(npow, hm_ref, nh)
        for _ in range(int(math.log2(c)) - 1):
            npow = _dot(npow, nbd)
            yield
            nbd = _stack_heads(npow, hm_ref, nh)
            tm = tm + _dot(tm, nbd)
            yield
        u = _dot(tm, _stack_heads(rhs, hm_ref, nh))
        yield
        o = am[c:2 * c] + _dot(jnp.concatenate([arb, ark], axis=1),
                               jnp.concatenate([_stack_heads(u, hm_ref, nh), vbd], axis=0))
        cl_last = cl[c - 1:c, :]
        tail = jnp.exp(cl_last - cl)
        upd = _dot_tn(jnp.concatenate([u, v], axis=0),
                      jnp.concatenate([bvec * tail, k * tail], axis=0))
        yield
        mt_ref[b] = mt * jnp.exp(cl_last) + upd * bd_ref[...]
        inv_n = 1.0 / RWKV_HEAD
        mean = _seg_dot(o, seg_ref[...]) * inv_n
        yield
        oc = o - mean
        var = _seg_dot(oc * oc, seg_ref[...]) * inv_n
        yield
        on = oc * lax.rsqrt(var + RWKV_GN_EPS) * lnw_ref[...] + lnb_ref[...]
        o_ref[b, r0:r0 + c, :] = (on + bonus) * g

    return [sequence(b) for b in range(p_ref.shape[0])]


def _rwkv_layer_consts(mu, w0, w_up, a0, a_up, g_up, k_k, k_a, r_k, ln_w, ln_b):
    row = lambda t: t.reshape(1, -1)
    lora_w = jnp.zeros((3, 128, MIX), F32)
    lora_w = lora_w.at[0, 0:32].set(w_up).at[1, 32:64].set(a_up).at[2, 64:128].set(g_up).astype(BF16)
    return (row(mu), row(w0), lora_w[0], row(a0), lora_w[1], lora_w[2], row(k_k), row(k_a),
            row(r_k), row(ln_w), row(ln_b))


def _mixers_kernel(*refs, counts):
    n_s5, n_rwkv, n_gla, n_rg = counts
    it = iter(refs)
    take = lambda k: [next(it) for _ in range(k)]
    u_ref, p_rwkv_ref, p_gla_ref, p_rg_ref = take(4)
    c_s5, c_rwkv, c_gla, c_rg = take(n_s5), take(n_rwkv), take(n_gla), take(n_rg)
    ya_ref, yb_ref, yc_ref, yd_ref = take(4)
    s_s5, s_rwkv, s_gla, s_rg = take(2), take(2), take(1), take(2)
    state_refs = [s_s5[0]] + s_rwkv + s_gla + s_rg

    @pl.when(pl.program_id(0) == 0)
    def _():
        for ref in state_refs:
            ref[...] = jnp.zeros_like(ref)

    rwkv = _rwkv_streams(p_rwkv_ref, *c_rwkv, yb_ref, *s_rwkv)
    gla = _gla_streams(p_gla_ref, *c_gla, yc_ref, *s_gla)
    rg = _rg_streams(p_rg_ref, *c_rg, yd_ref, *s_rg)
    s5 = [_s5_stream(u_ref, *c_s5, ya_ref, *s_s5)]
    gens = s5 + rwkv[:2] + rg[:2] + gla[:2] + rwkv[2:] + rg[2:] + gla[2:]
    _round_robin(gens)


def _mixer_layer_consts(bsz, s5_args, rwkv_args, gla_args, rg_args):
    return (jax.vmap(functools.partial(_s5_layer_consts, bsz))(*s5_args),
            jax.vmap(_rwkv_layer_consts)(*rwkv_args),
            jax.vmap(_gla_layer_consts)(*gla_args),
            jax.vmap(_rg_layer_consts)(*rg_args))


def _mixers(l, acts, layer_consts):
    bsz, seq, _ = acts[0].shape
    t = MIX_T
    shared = (_s5_shared(bsz), list(_rwkv_consts()), list(_gla_consts()), [])
    specs, operands, counts = [], [], []
    for lay, sh in zip(layer_consts, shared):
        specs += [_layer_spec(c, l) for c in lay] + [_const_spec(c.shape) for c in sh]
        operands += list(lay) + sh
        counts.append(len(lay) + len(sh))
    scratch = [pltpu.VMEM((2, SUB, S5_NSTATE), F32), pltpu.VMEM((bsz * t, 2 * S5_NSTATE), F32),
               pltpu.VMEM((bsz, 1, RWKV_WIDTH), F32), pltpu.VMEM((bsz, MIX, MIX), F32),
               pltpu.VMEM((bsz, MIX, GLA_KEY), F32),
               pltpu.VMEM((bsz, HALO + RG_T, MIX), F32), pltpu.VMEM((bsz, 1, MIX), F32)]
    seq_spec = lambda w: pl.BlockSpec((bsz, t, w), lambda i: (0, i, 0))
    return pl.pallas_call(
        functools.partial(_mixers_kernel, counts=tuple(counts)),
        grid=(seq // t,),
        in_specs=[seq_spec(a.shape[-1]) for a in acts] + specs,
        out_specs=[seq_spec(MIX)] * 4,
        out_shape=[jax.ShapeDtypeStruct((bsz, seq, MIX), F32)] * 4,
        scratch_shapes=scratch,
        compiler_params=_params(("arbitrary",)),
        name="mixers",
    )(*acts, *operands)


def _merge_kernel(x_ref, ya_ref, yb_ref, yc_ref, yd_ref, npre_ref, wg_ref, wb_ref, wo_ref,
                  npost_ref, o_ref):
    x = x_ref[...]
    h = _rms(x, npre_ref[...]).astype(BF16)
    merged = None
    for kbr, y_ref in enumerate((ya_ref, yb_ref, yc_ref, yd_ref)):
        gate = _sigmoid(jnp.dot(h, wg_ref[:, kbr * D_MODEL:(kbr + 1) * D_MODEL],
                                preferred_element_type=F32))
        term = gate * _dot(y_ref[...], wb_ref[kbr])
        merged = term if merged is None else merged + term
    m = _dot(merged, wo_ref[...])
    o_ref[...] = x + _rms(m, npost_ref[...])


def _merge(l, x2, ys, npre, wg, wb, wo, npost):
    n_tok = x2.shape[0]
    t = MERGE_T
    return pl.pallas_call(
        _merge_kernel,
        grid=(n_tok // t,),
        in_specs=[pl.BlockSpec((t, D_MODEL), lambda i: (i, 0))]
        + [pl.BlockSpec((t, MIX), lambda i: (i, 0))] * 4
        + [_layer_spec(a, l) for a in (npre, wg, wb, wo, npost)],
        out_specs=pl.BlockSpec((t, D_MODEL), lambda i: (i, 0)),
        out_shape=jax.ShapeDtypeStruct((n_tok, D_MODEL), F32),
        compiler_params=_params(("parallel",)),
        name="merge",
    )(x2, *ys, npre, wg, wb, wo, npost)


def _ffn_kernel(x_ref, npre_ref, win_ref, cw_ref, cb_ref, wout_ref, npost_ref, o_ref, gbuf_ref,
                *, tiles_per_seq):
    @pl.when(pl.program_id(0) % tiles_per_seq == 0)
    def _():
        gbuf_ref[0:HALO, :] = jnp.zeros((HALO, FFN_DIM), F32)

    t = FFN_T
    x = x_ref[...]
    h = _rms(x, npre_ref[...]).astype(BF16)
    gate = jnp.dot(h, win_ref[:, :FFN_DIM], preferred_element_type=F32)
    val = jnp.dot(h, win_ref[:, FFN_DIM:], preferred_element_type=F32)
    gbuf_ref[HALO:HALO + t, :] = gate
    gc = cb_ref[...] + cw_ref[FFN_CONV - 1:FFN_CONV, :] * gate
    for j in range(1, FFN_CONV):
        gc = gc + cw_ref[FFN_CONV - 1 - j:FFN_CONV - j, :] * gbuf_ref[HALO - j:HALO - j + t, :]
    gbuf_ref[0:HALO, :] = gate[t - HALO:t, :]
    f = _dot(jax.nn.gelu(gc) * val, wout_ref[...])
    o_ref[...] = x + _rms(f, npost_ref[...])


def _ffn(l, x2, seq, npre, w_in, conv_w, conv_b, w_out, npost):
    n_tok = x2.shape[0]
    t = FFN_T
    return pl.pallas_call(
        functools.partial(_ffn_kernel, tiles_per_seq=seq // t),
        grid=(n_tok // t,),
        in_specs=[pl.BlockSpec((t, D_MODEL), lambda i: (i, 0))]
        + [_layer_spec(a, l) for a in (npre, w_in, conv_w, conv_b, w_out, npost)],
        out_specs=pl.BlockSpec((t, D_MODEL), lambda i: (i, 0)),
        out_shape=jax.ShapeDtypeStruct((n_tok, D_MODEL), F32),
        scratch_shapes=[pltpu.VMEM((HALO + t, FFN_DIM), F32)],
        compiler_params=_params(("arbitrary",)),
        name="convffn",
    )(x2, npre, w_in, conv_w, conv_b, w_out, npost)


def kernel(x, norm_mix_pre, norm_mix_post, norm_ffn_pre, norm_ffn_post, w_in, s5_lambda_re, s5_lambda_im, s5_log_dt, s5_b_re, s5_b_im, s5_c_re, s5_c_im, s5_d, s5_w_glu, s5_b_glu, rwkv_mu, rwkv_w0, rwkv_w_up, rwkv_a0, rwkv_a_up, rwkv_g_up, rwkv_k_k, rwkv_k_a, rwkv_r_k, rwkv_ln_w, rwkv_ln_b, gla_alpha_up, gla_alpha_b, gla_norm_w, rg_conv_w, rg_conv_b, rg_w_a, rg_b_a, rg_w_x, rg_b_x, rg_lambda, w_branch, w_out, ffn_w_in, ffn_conv_w, ffn_conv_b, ffn_w_out):
    bsz, seq, _ = x.shape
    depth = w_in.shape[0]
    n_tok = bsz * seq
    x2 = x.reshape(n_tok, D_MODEL)
    o_rwkv = MIX
    o_gla = o_rwkv + 3 * MIX + 128
    o_rg = o_gla + GLA_WIDTH
    o_gate = o_rg + 2 * MIX
    w_mix = [w_in[:, :, :o_rwkv].astype(BF16), w_in[:, :, o_rwkv:o_gla].astype(BF16),
             jnp.pad(w_in[:, :, o_gla:o_rg],
                     ((0, 0), (0, 0), (0, GLA_WIDTH_PAD - GLA_WIDTH))).astype(BF16),
             w_in[:, :, o_rg:o_gate].astype(BF16)]
    w_gate = w_in[:, :, o_gate:].astype(BF16)
    w_branch_b, w_out_b = w_branch.astype(BF16), w_out.astype(BF16)
    ffn_w_in_b, ffn_w_out_b = ffn_w_in.astype(BF16), ffn_w_out.astype(BF16)
    vec = lambda t: t.reshape(depth, 1, t.shape[-1])
    npre, npost, fpre, fpost = (vec(norm_mix_pre), vec(norm_mix_post), vec(norm_ffn_pre),
                                vec(norm_ffn_post))
    mixer_consts = _mixer_layer_consts(
        bsz,
        (s5_lambda_re, s5_lambda_im, s5_log_dt, s5_b_re, s5_b_im, s5_c_re, s5_c_im, s5_d, s5_w_glu,
         s5_b_glu),
        (rwkv_mu, rwkv_w0, rwkv_w_up, rwkv_a0, rwkv_a_up, rwkv_g_up, rwkv_k_k, rwkv_k_a, rwkv_r_k,
         rwkv_ln_w, rwkv_ln_b),
        (gla_alpha_up, gla_alpha_b, gla_norm_w),
        (rg_conv_w, rg_conv_b, rg_w_a, rg_b_a, rg_w_x, rg_b_x, rg_lambda))
    for l in range(depth):
        ps = _proj(l, x2, npre, w_mix)
        ys = _mixers(l, [t.reshape(bsz, seq, t.shape[-1]) for t in ps], mixer_consts)
        ys = [t.reshape(n_tok, MIX) for t in ys]
        x2 = _merge(l, x2, ys, npre, w_gate, w_branch_b, w_out_b, npost)
        x2 = _ffn(l, x2, seq, fpre, ffn_w_in_b, ffn_conv_w, vec(ffn_conv_b), ffn_w_out_b, fpost)
    return x2.reshape(bsz, seq, D_MODEL)
```

```python
import functools
import math

import numpy as np
import jax
import jax.numpy as jnp
from jax import lax
from jax.experimental import pallas as pl
from jax.experimental.pallas import tpu as pltpu

F32 = jnp.float32
BF16 = jnp.bfloat16

D_MODEL = 1024
MIX = 256
NORM_EPS = 1e-6
S5_GROUPS, S5_GROUP, S5_STATE = 16, 16, 64
S5_NSTATE = S5_GROUPS * S5_STATE
S5_DT_MIN, S5_DT_MAX = 1e-3, 1e-1
RWKV_HEADS, RWKV_HEAD = 4, 64
RWKV_WIDTH = 3 * MIX + 128
RWKV_GN_EPS = 64e-5
GLA_HEADS, GLA_DK, GLA_DV = 4, 32, 64
GLA_KEY = GLA_HEADS * GLA_DK
GLA_GATE_LORA = 16
GLA_GATE_NORM = 16.0
GLA_WIDTH = 2 * GLA_KEY + 2 * MIX + GLA_GATE_LORA
GLA_WIDTH_PAD = 896
RG_CONV = 4
RG_C = 8.0
FFN_DIM = 2816
FFN_CONV = 3
HALO = 8

MIX_T = 128
S5_T = RG_T = GLA_T = MIX_T
RWKV_C = 64
SUB = 8
PROJ_T = 1024
MERGE_T = 512
FFN_T = 512

VMEM_LIMIT = 56 * 1024 * 1024


def _dot(a, b):
    return jnp.dot(a.astype(BF16), b.astype(BF16), preferred_element_type=F32)


def _dot_nt(a, b):
    return lax.dot_general(a.astype(BF16), b.astype(BF16), (((1,), (1,)), ((), ())),
                           preferred_element_type=F32)


def _dot_tn(a, b):
    return lax.dot_general(a.astype(BF16), b.astype(BF16), (((0,), (0,)), ((), ())),
                           preferred_element_type=F32)


def _split3(x):
    hi = x.astype(BF16)
    r1 = x - hi.astype(F32)
    mid = r1.astype(BF16)
    lo = (r1 - mid.astype(F32)).astype(BF16)
    return hi, mid, lo


def _sel_dot(c, x):
    hi, mid, lo = _split3(x)
    return (jnp.dot(c, hi, preferred_element_type=F32)
            + jnp.dot(c, mid, preferred_element_type=F32)
            + jnp.dot(c, lo, preferred_element_type=F32))


def _seg_dot(x, j):
    hi = x.astype(BF16)
    lo = (x - hi.astype(F32)).astype(BF16)
    m = x.shape[0]
    r = jnp.dot(jnp.concatenate([hi, lo], axis=0), j, preferred_element_type=F32)
    return r[0:m] + r[m:2 * m]


def _seg_dot_left(c, x):
    hi = x.astype(BF16)
    lo = (x - hi.astype(F32)).astype(BF16)
    n = x.shape[1]
    r = jnp.dot(c, jnp.concatenate([hi, lo], axis=1), preferred_element_type=F32)
    return r[:, 0:n] + r[:, n:2 * n]


def _rms(x, w):
    return x * lax.rsqrt(jnp.mean(x * x, axis=-1, keepdims=True) + NORM_EPS) * w


def _softplus(y):
    return jnp.maximum(y, 0.0) + jnp.log1p(jnp.exp(-jnp.abs(y)))


def _shift_rows(x, s, fill):
    rolled = pltpu.roll(x, s, axis=0)
    row = lax.broadcasted_iota(jnp.int32, x.shape, 0)
    return jnp.where(row >= s, rolled, fill)


def _stack_heads(x, hm_ref, nheads):
    xb = x.astype(BF16)
    return jnp.concatenate([xb * hm_ref[h:h + 1, :].astype(BF16) for h in range(nheads)], axis=0)


def _round_robin(gens):
    while gens:
        alive = []
        for gen in gens:
            try:
                next(gen)
                alive.append(gen)
            except StopIteration:
                pass
        gens = alive


def _const_spec(shape):
    nd = len(shape)
    return pl.BlockSpec(shape, lambda *_: (0,) * nd, pipeline_mode=pl.Buffered(1))


def _layer_spec(arr, l):
    nd = arr.ndim
    return pl.BlockSpec((None,) + arr.shape[1:], lambda *_: (l,) + (0,) * (nd - 1),
                        pipeline_mode=pl.Buffered(1))


def _sigmoid(x):
    return 0.5 * jnp.tanh(0.5 * x) + 0.5


def _params(sem):
    return pltpu.CompilerParams(dimension_semantics=sem, vmem_limit_bytes=VMEM_LIMIT)


def _proj_kernel(x_ref, nw_ref, *refs):
    n = len(refs) // 2
    h = _rms(x_ref[...], nw_ref[...]).astype(BF16)
    for w_ref, o_ref in zip(refs[:n], refs[n:]):
        o_ref[...] = jnp.dot(h, w_ref[...], preferred_element_type=F32)


def _proj(l, x2, nw, ws):
    n_tok = x2.shape[0]
    widths = [w.shape[-1] for w in ws]
    return pl.pallas_call(
        _proj_kernel,
        grid=(n_tok // PROJ_T,),
        in_specs=[pl.BlockSpec((PROJ_T, D_MODEL), lambda i: (i, 0)), _layer_spec(nw, l)]
        + [_layer_spec(w, l) for w in ws],
        out_specs=[pl.BlockSpec((PROJ_T, wd), lambda i: (i, 0)) for wd in widths],
        out_shape=[jax.ShapeDtypeStruct((n_tok, wd), F32) for wd in widths],
        compiler_params=_params(("parallel",)),
        name="proj",
    )(x2, nw, *ws)


def _cmul_add(hre, him, ar, ai, sre, sim):
    return hre + ar * sre - ai * sim, him + ar * sim + ai * sre


def _block_ends(x, nblk):
    return jnp.concatenate([x[j * SUB + SUB - 1:(j + 1) * SUB] for j in range(nblk)], axis=0)


S5_YIELD_EVERY = 4


def _s5_stream(u_ref, bmat_ref, are_ref, aim_ref, pwre_ref, pwim_ref, cmat_ref, d_ref, wglu_ref,
               bglu_ref, perm_ref, permt_ref, o_ref, h0_ref, hbuf_ref):
    bsz, t = u_ref.shape[0], S5_T
    nseg = SUB // bsz
    seglen = t // nseg
    rows = bsz * t
    ns = S5_NSTATE
    u = u_ref[...].reshape(rows, MIX)
    up = jnp.dot(perm_ref[...], u.astype(BF16), preferred_element_type=F32).astype(BF16)
    hbuf_ref[...] = jnp.dot(up, bmat_ref[...], preferred_element_type=F32)
    yield
    are = are_ref[...]
    aim = aim_ref[...]
    hre = h0_ref[0]
    him = h0_ref[1]
    for tt in range(seglen):
        r0 = tt * SUB
        hre, him = _cmul_add(hbuf_ref[r0:r0 + SUB, 0:ns], hbuf_ref[r0:r0 + SUB, ns:2 * ns],
                             are, aim, hre, him)
        hbuf_ref[r0:r0 + SUB, 0:ns] = hre
        hbuf_ref[r0:r0 + SUB, ns:2 * ns] = him
        if tt % S5_YIELD_EVERY == S5_YIELD_EVERY - 1:
            yield
    seg = lax.broadcasted_iota(jnp.int32, (SUB, ns), 0) & (nseg - 1)
    later = seg >= 1
    pre = pwre_ref[rows - SUB:rows, :]
    pim = pwim_ref[rows - SUB:rows, :]
    tre, tim = hre, him
    for _ in range(nseg - 1):
        sre = jnp.where(later, pltpu.roll(tre, 1, axis=0), 0.0)
        sim = jnp.where(later, pltpu.roll(tim, 1, axis=0), 0.0)
        tre, tim = _cmul_add(hre, him, pre, pim, sre, sim)
    cre = jnp.where(later, pltpu.roll(tre, 1, axis=0), 0.0)
    cim = jnp.where(later, pltpu.roll(tim, 1, axis=0), 0.0)
    first = seg == 0
    h0_ref[0] = jnp.where(first, pltpu.roll(tre, SUB - (nseg - 1), axis=0), 0.0)
    h0_ref[1] = jnp.where(first, pltpu.roll(tim, SUB - (nseg - 1), axis=0), 0.0)
    for tt in range(seglen):
        r0 = tt * SUB
        fre, fim = _cmul_add(hbuf_ref[r0:r0 + SUB, 0:ns], hbuf_ref[r0:r0 + SUB, ns:2 * ns],
                             pwre_ref[r0:r0 + SUB, :], pwim_ref[r0:r0 + SUB, :], cre, cim)
        hbuf_ref[r0:r0 + SUB, 0:ns] = fre
        hbuf_ref[r0:r0 + SUB, ns:2 * ns] = fim
        if tt % S5_YIELD_EVERY == S5_YIELD_EVERY - 1:
            yield
    yp = _dot(hbuf_ref[...], cmat_ref[...])
    yield
    y = _seg_dot_left(permt_ref[...], yp) + d_ref[...] * u
    yield
    y = jax.nn.gelu(y)
    y = y * jax.nn.sigmoid(_dot(y, wglu_ref[...]) + bglu_ref[...])
    o_ref[...] = y.reshape(bsz, t, MIX)


def _s5_shared(bsz):
    rows = bsz * S5_T
    seglen = rows // SUB
    src = (np.arange(rows) % SUB) * seglen + np.arange(rows) // SUB
    perm = np.zeros((rows, rows), np.float32)
    perm[np.arange(rows), src] = 1.0
    return [jnp.asarray(perm, BF16), jnp.asarray(perm.T, BF16)]


def _s5_layer_consts(bsz, lam_re, lam_im, log_dt, b_re, b_im, c_re, c_im, d, w_glu, b_glu):
    dt = jnp.exp(log_dt).reshape(S5_GROUPS, 1)
    mag = jnp.exp(lam_re * dt)
    ang = lam_im * dt
    ab_re = mag * jnp.cos(ang)
    ab_im = mag * jnp.sin(ang)
    den = lam_re * lam_re + lam_im * lam_im
    f_re = ((ab_re - 1.0) * lam_re + ab_im * lam_im) / den
    f_im = (ab_im * lam_re - (ab_re - 1.0) * lam_im) / den
    f_re = f_re.reshape(S5_GROUPS, S5_STATE, 1)
    f_im = f_im.reshape(S5_GROUPS, S5_STATE, 1)
    bb_re = f_re * b_re - f_im * b_im
    bb_im = f_re * b_im + f_im * b_re
    eye = jnp.eye(S5_GROUPS, dtype=F32)
    bm_re = jnp.einsum('gpc,gh->gchp', bb_re, eye).reshape(MIX, S5_NSTATE)
    bm_im = jnp.einsum('gpc,gh->gchp', bb_im, eye).reshape(MIX, S5_NSTATE)
    bmat = jnp.concatenate([bm_re, bm_im], axis=1).astype(BF16)
    cm_re = jnp.einsum('gcp,gh->gphc', c_re, eye).reshape(S5_NSTATE, MIX)
    cm_im = jnp.einsum('gcp,gh->gphc', c_im, eye).reshape(S5_NSTATE, MIX)
    cmat = jnp.concatenate([cm_re, -cm_im], axis=0).astype(BF16)
    assert SUB % bsz == 0
    seglen = S5_T * bsz // SUB
    pw_re = ab_re.reshape(1, S5_NSTATE)
    pw_im = ab_im.reshape(1, S5_NSTATE)
    while pw_re.shape[0] < seglen:
        top_re, top_im = pw_re[-1:], pw_im[-1:]
        pw_re, pw_im = (jnp.concatenate([pw_re, pw_re * top_re - pw_im * top_im], axis=0),
                        jnp.concatenate([pw_im, pw_re * top_im + pw_im * top_re], axis=0))
    return (bmat, pw_re, pw_im, cmat, d.reshape(1, MIX), w_glu.astype(BF16), b_glu.reshape(1, MIX))


def _rg_streams(p_ref, cw_ref, cb_ref, wa_ref, ba_ref, wx_ref, bx_ref, sp_ref,
                o_ref, xbuf_ref, h_ref):
    first = pl.program_id(0) == 0
    t, nblk = RG_T, RG_T // SUB
    row = lax.broadcasted_iota(jnp.int32, (t, MIX), 0)
    sub_row = lax.broadcasted_iota(jnp.int32, (nblk, SUB, MIX), 1)

    def sequence(bi):
        p = p_ref[bi]
        x = p[:, :MIX]
        y = p[:, MIX:]
        xbuf_ref[bi, HALO:HALO + t, :] = x
        xc = cb_ref[...] + cw_ref[RG_CONV - 1:RG_CONV, :] * x
        for j in range(1, RG_CONV):
            xc = xc + (cw_ref[RG_CONV - 1 - j:RG_CONV - j, :]
                       * xbuf_ref[bi, HALO - j:HALO - j + t, :])
        xbuf_ref[bi, 0:HALO, :] = x[t - HALO:t, :]
        za = _dot(xc, wa_ref[...])
        zx = _dot(xc, wx_ref[...])
        yield
        gate_r = jax.nn.sigmoid(za + ba_ref[...])
        gate_i = jax.nn.sigmoid(zx + bx_ref[...])
        log_a = -RG_C * gate_r * sp_ref[...]
        a = jnp.exp(log_a)
        th = jnp.tanh(log_a)
        mult = jnp.sqrt(-2.0 * th / (1.0 - th))
        mult = jnp.where(jnp.logical_and(first, row == 0), 1.0, mult)
        b = mult * (gate_i * xc)
        a = a.reshape(nblk, SUB, MIX)
        b = b.reshape(nblk, SUB, MIX)
        for i in range(int(math.log2(SUB))):
            live = sub_row >= 2 ** i
            b = b + a * jnp.where(live, pltpu.roll(b, 2 ** i, axis=1), 0.0)
            a = a * jnp.where(live, pltpu.roll(a, 2 ** i, axis=1), 1.0)
        a = a.reshape(t, MIX)
        b = b.reshape(t, MIX)
        yield
        ea = _block_ends(a, nblk)
        eb = _block_ends(b, nblk)
        s = 1
        while s < nblk:
            eb = eb + ea * _shift_rows(eb, s, 0.0)
            ea = ea * _shift_rows(ea, s, 1.0)
            s *= 2
        h_in = h_ref[bi]
        c = _shift_rows(eb, 1, 0.0) + _shift_rows(ea, 1, 1.0) * h_in
        h_ref[bi] = eb[nblk - 1:nblk] + ea[nblk - 1:nblk] * h_in
        hs = [b[j * SUB:(j + 1) * SUB] + a[j * SUB:(j + 1) * SUB] * c[j:j + 1] for j in range(nblk)]
        o_ref[bi] = jnp.concatenate(hs, axis=0) * jax.nn.gelu(y)

    return [sequence(bi) for bi in range(p_ref.shape[0])]


def _block_diag(w):
    nh, n, _ = w.shape
    eye = jnp.eye(nh, dtype=w.dtype)
    return jnp.einsum('hij,hg->higj', w, eye).reshape(nh * n, nh * n)


def _rg_layer_consts(conv_w, conv_b, w_a, b_a, w_x, b_x, lam):
    sp = jax.nn.softplus(-lam).reshape(1, MIX)
    row = lambda t: t.reshape(1, MIX)
    return (conv_w, row(conv_b), _block_diag(w_a).astype(BF16), row(b_a),
            _block_diag(w_x).astype(BF16), row(b_x), sp)


GLA_LEVELS = int(math.log2(GLA_T)) + 1


def _gla_consts():
    t = GLA_T
    tri = np.tril(np.ones((t, t), np.float32))
    cs = [tri]
    masks = [np.eye(t, dtype=np.float32)]
    idx = np.arange(t)
    for lvl in range(1, GLA_LEVELS):
        s = 2 ** (lvl - 1)
        mid = (idx // (2 * s)) * (2 * s) + s
        cs.append(tri[mid - 1] - tri)
        masks.append(((idx[:, None] ^ idx[None, :]) < 2 * s).astype(np.float32))
    cs.append(tri[t - 1:t] - tri)
    hm_k = np.zeros((GLA_HEADS, GLA_KEY), np.float32)
    hm_v = np.zeros((GLA_HEADS, MIX), np.float32)
    for h in range(GLA_HEADS):
        hm_k[h, h * GLA_DK:(h + 1) * GLA_DK] = 1.0
        hm_v[h, h * GLA_DV:(h + 1) * GLA_DV] = 1.0
    bd = hm_v.T @ hm_k
    seg = (hm_v.T @ hm_v) / GLA_DV
    return (jnp.asarray(np.concatenate(cs, axis=0), BF16), jnp.asarray(np.stack(masks), F32),
            jnp.asarray(hm_k), jnp.asarray(hm_v), jnp.asarray(bd), jnp.asarray(seg, BF16))


def _gla_streams(p_ref, aup_ref, ab_ref, nw_ref, cs_ref, mask_ref, hmk_ref, hmv_ref, bd_ref,
                 seg_ref, o_ref, st_ref):
    t = GLA_T
    row = lax.broadcasted_iota(jnp.int32, (t, GLA_KEY), 0)

    def sequence(b):
        p = p_ref[b]
        q = p[:, 0:GLA_KEY] * (GLA_DK ** -0.5)
        k = p[:, GLA_KEY:2 * GLA_KEY]
        v = p[:, 2 * GLA_KEY:2 * GLA_KEY + MIX]
        g = p[:, 2 * GLA_KEY + MIX:2 * GLA_KEY + 2 * MIX]
        lora = p[:, 2 * GLA_KEY + 2 * MIX:]
        z = _dot(lora, aup_ref[...]) + ab_ref[...]
        yield
        gk = -_softplus(-z) / GLA_GATE_NORM
        br = _sel_dot(cs_ref[...], gk)
        yield
        bc = br[0:t]
        st = st_ref[b]
        o_inter = _dot_nt(q * jnp.exp(bc), st)
        kd = k * jnp.exp(br[(GLA_LEVELS) * t:(GLA_LEVELS + 1) * t])
        st_ref[b] = st * jnp.exp(bc[t - 1:t, :]) + _dot_tn(v, kd) * bd_ref[...]
        attn = [None] * GLA_HEADS
        for lvl in range(GLA_LEVELS):
            if lvl == 0:
                qt, kt = q, k
            else:
                e = jnp.exp(-jnp.abs(br[lvl * t:(lvl + 1) * t]))
                upper = (row & (2 ** (lvl - 1))) != 0
                qt = jnp.where(upper, q * e, 0.0)
                kt = jnp.where(upper, 0.0, k * e)
            pr = _dot_nt(_stack_heads(qt, hmk_ref, GLA_HEADS), kt)
            yield
            m = mask_ref[lvl]
            for h in range(GLA_HEADS):
                term = pr[h * t:(h + 1) * t] * m
                attn[h] = term if attn[h] is None else attn[h] + term
        ov = _dot(jnp.concatenate(attn, axis=0), v)
        yield
        o = o_inter + ov[0:t] * hmv_ref[0:1, :]
        for h in range(1, GLA_HEADS):
            o = o + ov[h * t:(h + 1) * t] * hmv_ref[h:h + 1, :]
        ms = _seg_dot(o * o, seg_ref[...])
        yield
        o = o * lax.rsqrt(ms + NORM_EPS)
        o_ref[b] = o * nw_ref[...] * (g * jax.nn.sigmoid(g))

    return [sequence(b) for b in range(p_ref.shape[0])]


def _gla_layer_consts(alpha_up, alpha_b, norm_w):
    lora_tile = GLA_WIDTH_PAD - 2 * GLA_KEY - 2 * MIX
    aup = jnp.pad(alpha_up, ((0, lora_tile - GLA_GATE_LORA), (0, 0))).astype(BF16)
    return aup, alpha_b.reshape(1, GLA_KEY), norm_w.reshape(1, MIX)


def _rwkv_consts():
    c, nh, hd = RWKV_C, RWKV_HEADS, RWKV_HEAD
    hm = np.zeros((nh, MIX), np.float32)
    for h in range(nh):
        hm[h, h * hd:(h + 1) * hd] = 1.0
    tri = np.tril(np.ones((c, c), np.float32))
    eye = np.eye(c, dtype=np.float32)
    incl = np.tile(tri, (1, nh))
    strict = np.tile(tri - eye, (1, nh))
    ident = np.tile(eye, (1, nh))
    seg = hm.T @ hm
    return (jnp.asarray(hm), jnp.asarray(tri, BF16), jnp.asarray(strict), jnp.asarray(incl),
            jnp.asarray(ident), jnp.asarray(seg, BF16), jnp.asarray(seg))


def _rwkv_streams(p_ref, mu_ref, w0_ref, wup_ref, a0_ref, aup_ref, gup_ref, kk_ref, ka_ref,
                  rk_ref, lnw_ref, lnb_ref, hm_ref, tri_ref, strict_ref, incl_ref, ident_ref,
                  seg_ref, bd_ref, o_ref, prev_ref, mt_ref):
    c, nh = RWKV_C, RWKV_HEADS
    n = nh * c
    row = lax.broadcasted_iota(jnp.int32, (c, RWKV_WIDTH), 0)

    def sequence(b):
        for r0 in range(0, p_ref.shape[1], c):
            yield from chunk(b, r0)

    def chunk(b, r0):
        p0 = p_ref[b, r0:r0 + c, :]
        shifted = jnp.where(row == 0, prev_ref[b], pltpu.roll(p0, 1, axis=0))
        prev_ref[b] = p0[c - 1:c, :]
        p = p0 + (shifted - p0) * mu_ref[...]
        r = p[:, 0:MIX]
        k = p[:, MIX:2 * MIX]
        v = p[:, 2 * MIX:3 * MIX]
        lora = p[:, 3 * MIX:]
        w = -_softplus(-(w0_ref[...] + _dot(jnp.tanh(lora), wup_ref[...]))) - 0.5
        wlog = -jnp.exp(w)
        a = jax.nn.sigmoid(a0_ref[...] + _dot(lora, aup_ref[...]))
        g = _dot(jax.nn.sigmoid(lora), gup_ref[...])
        yield
        kk = k * kk_ref[...]
        k = k * (1.0 + (a - 1.0) * ka_ref[...])
        sums = _seg_dot(jnp.concatenate([kk * kk, r * k * rk_ref[...]], axis=0), seg_ref[...])
        cl = _sel_dot(tri_ref[...], wlog)
        yield
        kk = kk / jnp.maximum(jnp.sqrt(sums[0:c]), 1e-12)
        bonus = sums[c:2 * c] * v
        bvec = kk * a
        g_inv = jnp.exp(-cl)
        at = -kk * jnp.exp(cl - wlog)
        rt = r * jnp.exp(cl)
        x = jnp.concatenate([at, rt], axis=0)
        y = jnp.concatenate([_stack_heads(bvec * g_inv, hm_ref, nh),
                             _stack_heads(k * g_inv, hm_ref, nh)], axis=0)
        pr = _dot_nt(x, y)
        mt = mt_ref[b]
        am = _dot_nt(x, mt)
        yield
        strict = strict_ref[...]
        incl = incl_ref[...]
        nab = pr[0:c, 0:n] * strict
        aak = pr[0:c, n:2 * n] * strict
        arb = pr[c:2 * c, 0:n] * incl
        ark = pr[c:2 * c, n:2 * n] * incl
        vbd = _stack_heads(v, hm_ref, nh)
        rhs = _dot(aak, vbd) + am[0:c]
        tm = ident_ref[...] + nab
        npow = nab
        nbd = _stack_heads(npow, hm_ref, nh)
        for _ in range(int(math.log2(c)) - 1):
            npow = _dot(npow, nbd)
            yield
            nbd = _stack_heads(npow, hm_ref, nh)
            tm = tm + _dot(tm, nbd)
            yield
        u = _dot(tm, _stack_heads(rhs, hm_ref, nh))
        yield
        o = am[c:2 * c] + _dot(jnp.concatenate([arb, ark], axis=1),
                               jnp.concatenate([_stack_heads(u, hm_ref, nh), vbd], axis=0))
        cl_last = cl[c - 1:c, :]
        tail = jnp.exp(cl_last - cl)
        upd = _dot_tn(jnp.concatenate([u, v], axis=0),
                      jnp.concatenate([bvec * tail, k * tail], axis=0))
        yield
        mt_ref[b] = mt * jnp.exp(cl_last) + upd * bd_ref[...]
        inv_n = 1.0 / RWKV_HEAD
        mean = _seg_dot(o, seg_ref[...]) * inv_n
        yield
        oc = o - mean
        var = _seg_dot(oc * oc, seg_ref[...]) * inv_n
        yield
        on = oc * lax.rsqrt(var + RWKV_GN_EPS) * lnw_ref[...] + lnb_ref[...]
        o_ref[b, r0:r0 + c, :] = (on + bonus) * g

    return [sequence(b) for b in range(p_ref.shape[0])]


def _rwkv_layer_consts(mu, w0, w_up, a0, a_up, g_up, k_k, k_a, r_k, ln_w, ln_b):
    row = lambda t: t.reshape(1, -1)
    place = lambda w, lo: jnp.pad(w, ((lo, 128 - lo - w.shape[0]), (0, 0))).astype(BF16)
    return (row(mu), row(w0), place(w_up, 0), row(a0), place(a_up, 32), place(g_up, 64), row(k_k),
            row(k_a), row(r_k), row(ln_w), row(ln_b))


def _mixers_kernel(*refs, counts):
    n_s5, n_rwkv, n_gla, n_rg = counts
    it = iter(refs)
    take = lambda k: [next(it) for _ in range(k)]
    u_ref, p_rwkv_ref, p_gla_ref, p_rg_ref = take(4)
    c_s5, c_rwkv, c_gla, c_rg = take(n_s5), take(n_rwkv), take(n_gla), take(n_rg)
    ya_ref, yb_ref, yc_ref, yd_ref = take(4)
    s_s5, s_rwkv, s_gla, s_rg = take(2), take(2), take(1), take(2)
    state_refs = [s_s5[0]] + s_rwkv + s_gla + s_rg

    @pl.when(pl.program_id(0) == 0)
    def _():
        for ref in state_refs:
            ref[...] = jnp.zeros_like(ref)

    rwkv = _rwkv_streams(p_rwkv_ref, *c_rwkv, yb_ref, *s_rwkv)
    gla = _gla_streams(p_gla_ref, *c_gla, yc_ref, *s_gla)
    rg = _rg_streams(p_rg_ref, *c_rg, yd_ref, *s_rg)
    s5 = [_s5_stream(u_ref, *c_s5, ya_ref, *s_s5)]
    gens = s5 + rwkv[:2] + rg[:2] + gla[:2] + rwkv[2:] + rg[2:] + gla[2:]
    _round_robin(gens)


def _mixer_layer_consts(bsz, s5_args, rwkv_args, gla_args, rg_args):
    bmat, pw_re, pw_im, *rest = jax.vmap(functools.partial(_s5_layer_consts, bsz))(*s5_args)
    pw_re = jnp.repeat(pw_re, SUB, axis=1)
    pw_im = jnp.repeat(pw_im, SUB, axis=1)
    s5 = (bmat, pw_re[:, 0:SUB], pw_im[:, 0:SUB], pw_re, pw_im, *rest)
    return (s5,
            jax.vmap(_rwkv_layer_consts)(*rwkv_args),
            jax.vmap(_gla_layer_consts)(*gla_args),
            jax.vmap(_rg_layer_consts)(*rg_args))


def _mixers(l, acts, layer_consts):
    bsz, seq, _ = acts[0].shape
    t = MIX_T
    shared = (_s5_shared(bsz), list(_rwkv_consts()), list(_gla_consts()), [])
    specs, operands, counts = [], [], []
    for lay, sh in zip(layer_consts, shared):
        specs += [_layer_spec(c, l) for c in lay] + [_const_spec(c.shape) for c in sh]
        operands += list(lay) + sh
        counts.append(len(lay) + len(sh))
    scratch = [pltpu.VMEM((2, SUB, S5_NSTATE), F32), pltpu.VMEM((bsz * t, 2 * S5_NSTATE), F32),
               pltpu.VMEM((bsz, 1, RWKV_WIDTH), F32), pltpu.VMEM((bsz, MIX, MIX), F32),
               pltpu.VMEM((bsz, MIX, GLA_KEY), F32),
               pltpu.VMEM((bsz, HALO + RG_T, MIX), F32), pltpu.VMEM((bsz, 1, MIX), F32)]
    seq_spec = lambda w: pl.BlockSpec((bsz, t, w), lambda i: (0, i, 0))
    return pl.pallas_call(
        functools.partial(_mixers_kernel, counts=tuple(counts)),
        grid=(seq // t,),
        in_specs=[seq_spec(a.shape[-1]) for a in acts] + specs,
        out_specs=[seq_spec(MIX)] * 4,
        out_shape=[jax.ShapeDtypeStruct((bsz, seq, MIX), F32)] * 4,
        scratch_shapes=scratch,
        compiler_params=_params(("arbitrary",)),
        name="mixers",
    )(*acts, *operands)


def _merge_kernel(x_ref, ya_ref, yb_ref, yc_ref, yd_ref, npre_ref, wg_ref, wb_ref, wo_ref,
                  npost_ref, o_ref):
    x = x_ref[...]
    h = _rms(x, npre_ref[...]).astype(BF16)
    merged = None
    for kbr, y_ref in enumerate((ya_ref, yb_ref, yc_ref, yd_ref)):
        gate = _sigmoid(jnp.dot(h, wg_ref[:, kbr * D_MODEL:(kbr + 1) * D_MODEL],
                                preferred_element_type=F32))
        term = gate * _dot(y_ref[...], wb_ref[kbr])
        merged = term if merged is None else merged + term
    m = _dot(merged, wo_ref[...])
    o_ref[...] = x + _rms(m, npost_ref[...])


def _merge(l, x2, ys, npre, wg, wb, wo, npost):
    n_tok = x2.shape[0]
    t = MERGE_T
    return pl.pallas_call(
        _merge_kernel,
        grid=(n_tok // t,),
        in_specs=[pl.BlockSpec((t, D_MODEL), lambda i: (i, 0))]
        + [pl.BlockSpec((t, MIX), lambda i: (i, 0))] * 4
        + [_layer_spec(a, l) for a in (npre, wg, wb, wo, npost)],
        out_specs=pl.BlockSpec((t, D_MODEL), lambda i: (i, 0)),
        out_shape=jax.ShapeDtypeStruct((n_tok, D_MODEL), F32),
        compiler_params=_params(("parallel",)),
        name="merge",
    )(x2, *ys, npre, wg, wb, wo, npost)


def _ffn_kernel(x_ref, npre_ref, win_ref, cw_ref, cb_ref, wout_ref, npost_ref, o_ref, gbuf_ref,
                *, tiles_per_seq):
    @pl.when(pl.program_id(0) % tiles_per_seq == 0)
    def _():
        gbuf_ref[0:HALO, :] = jnp.zeros((HALO, FFN_DIM), F32)

    t = FFN_T
    x = x_ref[...]
    h = _rms(x, npre_ref[...]).astype(BF16)
    gate = jnp.dot(h, win_ref[:, :FFN_DIM], preferred_element_type=F32)
    val = jnp.dot(h, win_ref[:, FFN_DIM:], preferred_element_type=F32)
    gbuf_ref[HALO:HALO + t, :] = gate
    gc = cb_ref[...] + cw_ref[FFN_CONV - 1:FFN_CONV, :] * gate
    for j in range(1, FFN_CONV):
        gc = gc + cw_ref[FFN_CONV - 1 - j:FFN_CONV - j, :] * gbuf_ref[HALO - j:HALO - j + t, :]
    gbuf_ref[0:HALO, :] = gate[t - HALO:t, :]
    f = _dot(jax.nn.gelu(gc) * val, wout_ref[...])
    o_ref[...] = x + _rms(f, npost_ref[...])


def _ffn(l, x2, seq, npre, w_in, conv_w, conv_b, w_out, npost):
    n_tok = x2.shape[0]
    t = FFN_T
    return pl.pallas_call(
        functools.partial(_ffn_kernel, tiles_per_seq=seq // t),
        grid=(n_tok // t,),
        in_specs=[pl.BlockSpec((t, D_MODEL), lambda i: (i, 0))]
        + [_layer_spec(a, l) for a in (npre, w_in, conv_w, conv_b, w_out, npost)],
        out_specs=pl.BlockSpec((t, D_MODEL), lambda i: (i, 0)),
        out_shape=jax.ShapeDtypeStruct((n_tok, D_MODEL), F32),
        scratch_shapes=[pltpu.VMEM((HALO + t, FFN_DIM), F32)],
        compiler_params=_params(("arbitrary",)),
        name="convffn",
    )(x2, npre, w_in, conv_w, conv_b, w_out, npost)


def kernel(x, norm_mix_pre, norm_mix_post, norm_ffn_pre, norm_ffn_post, w_in, s5_lambda_re, s5_lambda_im, s5_log_dt, s5_b_re, s5_b_im, s5_c_re, s5_c_im, s5_d, s5_w_glu, s5_b_glu, rwkv_mu, rwkv_w0, rwkv_w_up, rwkv_a0, rwkv_a_up, rwkv_g_up, rwkv_k_k, rwkv_k_a, rwkv_r_k, rwkv_ln_w, rwkv_ln_b, gla_alpha_up, gla_alpha_b, gla_norm_w, rg_conv_w, rg_conv_b, rg_w_a, rg_b_a, rg_w_x, rg_b_x, rg_lambda, w_branch, w_out, ffn_w_in, ffn_conv_w, ffn_conv_b, ffn_w_out):
    bsz, seq, _ = x.shape
    depth = w_in.shape[0]
    n_tok = bsz * seq
    x2 = x.reshape(n_tok, D_MODEL)
    o_rwkv = MIX
    o_gla = o_rwkv + 3 * MIX + 128
    o_rg = o_gla + GLA_WIDTH
    o_gate = o_rg + 2 * MIX
    w_mix = [w_in[:, :, :o_rwkv].astype(BF16), w_in[:, :, o_rwkv:o_gla].astype(BF16),
             lax.pad(w_in[:, :, o_gla:o_rg], jnp.zeros((), w_in.dtype),
                     [(0, 0, 0), (0, 0, 0), (0, GLA_WIDTH_PAD - GLA_WIDTH, 0)]).astype(BF16),
             w_in[:, :, o_rg:o_gate].astype(BF16)]
    w_gate = w_in[:, :, o_gate:].astype(BF16)
    w_branch_b, w_out_b = w_branch.astype(BF16), w_out.astype(BF16)
    ffn_w_in_b, ffn_w_out_b = ffn_w_in.astype(BF16), ffn_w_out.astype(BF16)
    vec = lambda t: t.reshape(depth, 1, t.shape[-1])
    npre, npost, fpre, fpost = (vec(norm_mix_pre), vec(norm_mix_post), vec(norm_ffn_pre),
                                vec(norm_ffn_post))
    mixer_consts = _mixer_layer_consts(
        bsz,
        (s5_lambda_re, s5_lambda_im, s5_log_dt, s5_b_re, s5_b_im, s5_c_re, s5_c_im, s5_d, s5_w_glu,
         s5_b_glu),
        (rwkv_mu, rwkv_w0, rwkv_w_up, rwkv_a0, rwkv_a_up, rwkv_g_up, rwkv_k_k, rwkv_k_a, rwkv_r_k,
         rwkv_ln_w, rwkv_ln_b),
        (gla_alpha_up, gla_alpha_b, gla_norm_w),
        (rg_conv_w, rg_conv_b, rg_w_a, rg_b_a, rg_w_x, rg_b_x, rg_lambda))
    for l in range(depth):
        ps = _proj(l, x2, npre, w_mix)
        ys = _mixers(l, [t.reshape(bsz, seq, t.shape[-1]) for t in ps], mixer_consts)
        ys = [t.reshape(n_tok, MIX) for t in ys]
        x2 = _merge(l, x2, ys, npre, w_gate, w_branch_b, w_out_b, npost)
        x2 = _ffn(l, x2, seq, fpre, ffn_w_in_b, ffn_conv_w, vec(ffn_conv_b), ffn_w_out_b, fpost)
    return x2.reshape(bsz, seq, D_MODEL)
```

```python
import functools
import math

import numpy as np
import jax
import jax.numpy as jnp
from jax import lax
from jax.experimental import pallas as pl
from jax.experimental.pallas import tpu as pltpu

F32 = jnp.float32
BF16 = jnp.bfloat16

D_MODEL = 1024
MIX = 256
NORM_EPS = 1e-6
S5_GROUPS, S5_GROUP, S5_STATE = 16, 16, 64
S5_NSTATE = S5_GROUPS * S5_STATE
S5_DT_MIN, S5_DT_MAX = 1e-3, 1e-1
RWKV_HEADS, RWKV_HEAD = 4, 64
RWKV_WIDTH = 3 * MIX + 128
RWKV_GN_EPS = 64e-5
GLA_HEADS, GLA_DK, GLA_DV = 4, 32, 64
GLA_KEY = GLA_HEADS * GLA_DK
GLA_GATE_LORA = 16
GLA_GATE_NORM = 16.0
GLA_WIDTH = 2 * GLA_KEY + 2 * MIX + GLA_GATE_LORA
GLA_WIDTH_PAD = 896
RG_CONV = 4
RG_C = 8.0
FFN_DIM = 2816
FFN_CONV = 3
HALO = 8

MIX_T = 128
S5_T = RG_T = GLA_T = MIX_T
RWKV_C = 64
SUB = 8
PROJ_T = 1024
MERGE_T = 512
FFN_T = 512

VMEM_LIMIT = 56 * 1024 * 1024


def _dot(a, b):
    return jnp.dot(a.astype(BF16), b.astype(BF16), preferred_element_type=F32)


def _dot_nt(a, b):
    return lax.dot_general(a.astype(BF16), b.astype(BF16), (((1,), (1,)), ((), ())),
                           preferred_element_type=F32)


def _dot_tn(a, b):
    return lax.dot_general(a.astype(BF16), b.astype(BF16), (((0,), (0,)), ((), ())),
                           preferred_element_type=F32)


def _split3(x):
    hi = x.astype(BF16)
    r1 = x - hi.astype(F32)
    mid = r1.astype(BF16)
    lo = (r1 - mid.astype(F32)).astype(BF16)
    return hi, mid, lo


def _sel_dot(c, x):
    hi, mid, lo = _split3(x)
    return (jnp.dot(c, hi, preferred_element_type=F32)
            + jnp.dot(c, mid, preferred_element_type=F32)
            + jnp.dot(c, lo, preferred_element_type=F32))


def _seg_dot(x, j):
    hi = x.astype(BF16)
    lo = (x - hi.astype(F32)).astype(BF16)
    m = x.shape[0]
    r = jnp.dot(jnp.concatenate([hi, lo], axis=0), j, preferred_element_type=F32)
    return r[0:m] + r[m:2 * m]


def _seg_dot_left(c, x):
    hi = x.astype(BF16)
    lo = (x - hi.astype(F32)).astype(BF16)
    n = x.shape[1]
    r = jnp.dot(c, jnp.concatenate([hi, lo], axis=1), preferred_element_type=F32)
    return r[:, 0:n] + r[:, n:2 * n]


def _rms(x, w):
    return x * lax.rsqrt(jnp.mean(x * x, axis=-1, keepdims=True) + NORM_EPS) * w


def _softplus(y):
    return jnp.maximum(y, 0.0) + jnp.log1p(jnp.exp(-jnp.abs(y)))


def _shift_rows(x, s, fill):
    rolled = pltpu.roll(x, s, axis=0)
    row = lax.broadcasted_iota(jnp.int32, x.shape, 0)
    return jnp.where(row >= s, rolled, fill)


def _stack_heads(x, hm_ref, nheads):
    xb = x.astype(BF16)
    return jnp.concatenate([xb * hm_ref[h:h + 1, :].astype(BF16) for h in range(nheads)], axis=0)


def _round_robin(gens):
    while gens:
        alive = []
        for gen in gens:
            try:
                next(gen)
                alive.append(gen)
            except StopIteration:
                pass
        gens = alive


def _const_spec(shape):
    nd = len(shape)
    return pl.BlockSpec(shape, lambda *_: (0,) * nd, pipeline_mode=pl.Buffered(1))


def _layer_spec(arr, l):
    nd = arr.ndim
    return pl.BlockSpec((None,) + arr.shape[1:], lambda *_: (l,) + (0,) * (nd - 1),
                        pipeline_mode=pl.Buffered(1))


def _sigmoid(x):
    return 0.5 * jnp.tanh(0.5 * x) + 0.5


def _params(sem):
    return pltpu.CompilerParams(dimension_semantics=sem, vmem_limit_bytes=VMEM_LIMIT)


def _proj_kernel(x_ref, nw_ref, *refs):
    n = len(refs) // 2
    h = _rms(x_ref[...], nw_ref[...]).astype(BF16)
    for w_ref, o_ref in zip(refs[:n], refs[n:]):
        o_ref[...] = jnp.dot(h, w_ref[...], preferred_element_type=F32)


def _proj(l, x2, nw, ws):
    n_tok = x2.shape[0]
    widths = [w.shape[-1] for w in ws]
    return pl.pallas_call(
        _proj_kernel,
        grid=(n_tok // PROJ_T,),
        in_specs=[pl.BlockSpec((PROJ_T, D_MODEL), lambda i: (i, 0)), _layer_spec(nw, l)]
        + [_layer_spec(w, l) for w in ws],
        out_specs=[pl.BlockSpec((PROJ_T, wd), lambda i: (i, 0)) for wd in widths],
        out_shape=[jax.ShapeDtypeStruct((n_tok, wd), F32) for wd in widths],
        compiler_params=_params(("parallel",)),
        name="proj",
    )(x2, nw, *ws)


def _cmul_add(hre, him, ar, ai, sre, sim):
    return hre + ar * sre - ai * sim, him + ar * sim + ai * sre


def _block_ends(x, nblk):
    return jnp.concatenate([x[j * SUB + SUB - 1:(j + 1) * SUB] for j in range(nblk)], axis=0)


S5_YIELD_EVERY = 8


def _s5_stream(u_ref, bmat_ref, are_ref, aim_ref, pwre_ref, pwim_ref, cmat_ref, d_ref, wglu_ref,
               bglu_ref, perm_ref, permt_ref, o_ref, h0_ref, hbuf_ref):
    bsz, t = u_ref.shape[0], S5_T
    nseg = SUB // bsz
    seglen = t // nseg
    rows = bsz * t
    ns = S5_NSTATE
    u = u_ref[...].reshape(rows, MIX)
    up = jnp.dot(perm_ref[...], u.astype(BF16), preferred_element_type=F32).astype(BF16)
    hbuf_ref[...] = jnp.dot(up, bmat_ref[...], preferred_element_type=F32)
    yield
    are = are_ref[...]
    aim = aim_ref[...]
    hre = h0_ref[0]
    him = h0_ref[1]
    for tt in range(seglen):
        r0 = tt * SUB
        hre, him = _cmul_add(hbuf_ref[r0:r0 + SUB, 0:ns], hbuf_ref[r0:r0 + SUB, ns:2 * ns],
                             are, aim, hre, him)
        hbuf_ref[r0:r0 + SUB, 0:ns] = hre
        hbuf_ref[r0:r0 + SUB, ns:2 * ns] = him
        if tt % S5_YIELD_EVERY == S5_YIELD_EVERY - 1:
            yield
    seg = lax.broadcasted_iota(jnp.int32, (SUB, ns), 0) & (nseg - 1)
    later = seg >= 1
    pre = pwre_ref[rows - SUB:rows, :]
    pim = pwim_ref[rows - SUB:rows, :]
    tre, tim = hre, him
    for _ in range(nseg - 1):
        sre = jnp.where(later, pltpu.roll(tre, 1, axis=0), 0.0)
        sim = jnp.where(later, pltpu.roll(tim, 1, axis=0), 0.0)
        tre, tim = _cmul_add(hre, him, pre, pim, sre, sim)
    cre = jnp.where(later, pltpu.roll(tre, 1, axis=0), 0.0)
    cim = jnp.where(later, pltpu.roll(tim, 1, axis=0), 0.0)
    first = seg == 0
    h0_ref[0] = jnp.where(first, pltpu.roll(tre, SUB - (nseg - 1), axis=0), 0.0)
    h0_ref[1] = jnp.where(first, pltpu.roll(tim, SUB - (nseg - 1), axis=0), 0.0)
    for tt in range(seglen):
        r0 = tt * SUB
        fre, fim = _cmul_add(hbuf_ref[r0:r0 + SUB, 0:ns], hbuf_ref[r0:r0 + SUB, ns:2 * ns],
                             pwre_ref[r0:r0 + SUB, :], pwim_ref[r0:r0 + SUB, :], cre, cim)
        hbuf_ref[r0:r0 + SUB, 0:ns] = fre
        hbuf_ref[r0:r0 + SUB, ns:2 * ns] = fim
        if tt % S5_YIELD_EVERY == S5_YIELD_EVERY - 1:
            yield
    yp = _dot(hbuf_ref[...], cmat_ref[...])
    yield
    y = _seg_dot_left(permt_ref[...], yp) + d_ref[...] * u
    yield
    y = jax.nn.gelu(y)
    y = y * jax.nn.sigmoid(_dot(y, wglu_ref[...]) + bglu_ref[...])
    o_ref[...] = y.reshape(bsz, t, MIX)


def _s5_shared(bsz):
    rows = bsz * S5_T
    seglen = rows // SUB
    src = (np.arange(rows) % SUB) * seglen + np.arange(rows) // SUB
    perm = np.zeros((rows, rows), np.float32)
    perm[np.arange(rows), src] = 1.0
    return [jnp.asarray(perm, BF16), jnp.asarray(perm.T, BF16)]


def _s5_layer_consts(bsz, lam_re, lam_im, log_dt, b_re, b_im, c_re, c_im, d, w_glu, b_glu):
    dt = jnp.exp(log_dt).reshape(S5_GROUPS, 1)
    mag = jnp.exp(lam_re * dt)
    ang = lam_im * dt
    ab_re = mag * jnp.cos(ang)
    ab_im = mag * jnp.sin(ang)
    den = lam_re * lam_re + lam_im * lam_im
    f_re = ((ab_re - 1.0) * lam_re + ab_im * lam_im) / den
    f_im = (ab_im * lam_re - (ab_re - 1.0) * lam_im) / den
    f_re = f_re.reshape(S5_GROUPS, S5_STATE, 1)
    f_im = f_im.reshape(S5_GROUPS, S5_STATE, 1)
    bb_re = f_re * b_re - f_im * b_im
    bb_im = f_re * b_im + f_im * b_re
    eye = jnp.eye(S5_GROUPS, dtype=F32)
    bm_re = jnp.einsum('gpc,gh->gchp', bb_re, eye).reshape(MIX, S5_NSTATE)
    bm_im = jnp.einsum('gpc,gh->gchp', bb_im, eye).reshape(MIX, S5_NSTATE)
    bmat = jnp.concatenate([bm_re, bm_im], axis=1).astype(BF16)
    cm_re = jnp.einsum('gcp,gh->gphc', c_re, eye).reshape(S5_NSTATE, MIX)
    cm_im = jnp.einsum('gcp,gh->gphc', c_im, eye).reshape(S5_NSTATE, MIX)
    cmat = jnp.concatenate([cm_re, -cm_im], axis=0).astype(BF16)
    assert SUB % bsz == 0
    seglen = S5_T * bsz // SUB
    pw_re = ab_re.reshape(1, S5_NSTATE)
    pw_im = ab_im.reshape(1, S5_NSTATE)
    while pw_re.shape[0] < seglen:
        top_re, top_im = pw_re[-1:], pw_im[-1:]
        pw_re, pw_im = (jnp.concatenate([pw_re, pw_re * top_re - pw_im * top_im], axis=0),
                        jnp.concatenate([pw_im, pw_re * top_im + pw_im * top_re], axis=0))
    return (bmat, pw_re, pw_im, cmat, d.reshape(1, MIX), w_glu.astype(BF16), b_glu.reshape(1, MIX))


def _rg_streams(p_ref, cw_ref, cb_ref, wa_ref, ba_ref, wx_ref, bx_ref, sp_ref,
                o_ref, xbuf_ref, h_ref):
    first = pl.program_id(0) == 0
    t, nblk = RG_T, RG_T // SUB
    row = lax.broadcasted_iota(jnp.int32, (t, MIX), 0)
    sub_row = lax.broadcasted_iota(jnp.int32, (nblk, SUB, MIX), 1)

    def sequence(bi):
        p = p_ref[bi]
        x = p[:, :MIX]
        y = p[:, MIX:]
        xbuf_ref[bi, HALO:HALO + t, :] = x
        xc = cb_ref[...] + cw_ref[RG_CONV - 1:RG_CONV, :] * x
        for j in range(1, RG_CONV):
            xc = xc + (cw_ref[RG_CONV - 1 - j:RG_CONV - j, :]
                       * xbuf_ref[bi, HALO - j:HALO - j + t, :])
        xbuf_ref[bi, 0:HALO, :] = x[t - HALO:t, :]
        za = _dot(xc, wa_ref[...])
        zx = _dot(xc, wx_ref[...])
        yield
        gate_r = jax.nn.sigmoid(za + ba_ref[...])
        gate_i = jax.nn.sigmoid(zx + bx_ref[...])
        log_a = -RG_C * gate_r * sp_ref[...]
        a = jnp.exp(log_a)
        th = jnp.tanh(log_a)
        mult = jnp.sqrt(-2.0 * th / (1.0 - th))
        mult = jnp.where(jnp.logical_and(first, row == 0), 1.0, mult)
        b = mult * (gate_i * xc)
        a = a.reshape(nblk, SUB, MIX)
        b = b.reshape(nblk, SUB, MIX)
        for i in range(int(math.log2(SUB))):
            live = sub_row >= 2 ** i
            b = b + a * jnp.where(live, pltpu.roll(b, 2 ** i, axis=1), 0.0)
            a = a * jnp.where(live, pltpu.roll(a, 2 ** i, axis=1), 1.0)
        a = a.reshape(t, MIX)
        b = b.reshape(t, MIX)
        yield
        ea = _block_ends(a, nblk)
        eb = _block_ends(b, nblk)
        s = 1
        while s < nblk:
            eb = eb + ea * _shift_rows(eb, s, 0.0)
            ea = ea * _shift_rows(ea, s, 1.0)
            s *= 2
        h_in = h_ref[bi]
        c = _shift_rows(eb, 1, 0.0) + _shift_rows(ea, 1, 1.0) * h_in
        h_ref[bi] = eb[nblk - 1:nblk] + ea[nblk - 1:nblk] * h_in
        hs = [b[j * SUB:(j + 1) * SUB] + a[j * SUB:(j + 1) * SUB] * c[j:j + 1] for j in range(nblk)]
        o_ref[bi] = jnp.concatenate(hs, axis=0) * jax.nn.gelu(y)

    return [sequence(bi) for bi in range(p_ref.shape[0])]


def _block_diag(w):
    nh, n, _ = w.shape
    eye = jnp.eye(nh, dtype=w.dtype)
    return jnp.einsum('hij,hg->higj', w, eye).reshape(nh * n, nh * n)


def _rg_layer_consts(conv_w, conv_b, w_a, b_a, w_x, b_x, lam):
    sp = jax.nn.softplus(-lam).reshape(1, MIX)
    row = lambda t: t.reshape(1, MIX)
    return (conv_w, row(conv_b), _block_diag(w_a).astype(BF16), row(b_a),
            _block_diag(w_x).astype(BF16), row(b_x), sp)


GLA_LEVELS = int(math.log2(GLA_T)) + 1


def _gla_consts():
    t = GLA_T
    tri = np.tril(np.ones((t, t), np.float32))
    cs = [tri]
    masks = [np.eye(t, dtype=np.float32)]
    idx = np.arange(t)
    for lvl in range(1, GLA_LEVELS):
        s = 2 ** (lvl - 1)
        mid = (idx // (2 * s)) * (2 * s) + s
        cs.append(tri[mid - 1] - tri)
        masks.append(((idx[:, None] ^ idx[None, :]) < 2 * s).astype(np.float32))
    cs.append(tri[t - 1:t] - tri)
    hm_k = np.zeros((GLA_HEADS, GLA_KEY), np.float32)
    hm_v = np.zeros((GLA_HEADS, MIX), np.float32)
    for h in range(GLA_HEADS):
        hm_k[h, h * GLA_DK:(h + 1) * GLA_DK] = 1.0
        hm_v[h, h * GLA_DV:(h + 1) * GLA_DV] = 1.0
    bd = hm_v.T @ hm_k
    seg = (hm_v.T @ hm_v) / GLA_DV
    return (jnp.asarray(np.concatenate(cs, axis=0), BF16), jnp.asarray(np.stack(masks), F32),
            jnp.asarray(hm_k), jnp.asarray(hm_v), jnp.asarray(bd), jnp.asarray(seg, BF16))


def _gla_streams(p_ref, aup_ref, ab_ref, nw_ref, cs_ref, mask_ref, hmk_ref, hmv_ref, bd_ref,
                 seg_ref, o_ref, st_ref):
    t = GLA_T
    row = lax.broadcasted_iota(jnp.int32, (t, GLA_KEY), 0)

    def sequence(b):
        p = p_ref[b]
        q = p[:, 0:GLA_KEY] * (GLA_DK ** -0.5)
        k = p[:, GLA_KEY:2 * GLA_KEY]
        v = p[:, 2 * GLA_KEY:2 * GLA_KEY + MIX]
        g = p[:, 2 * GLA_KEY + MIX:2 * GLA_KEY + 2 * MIX]
        lora = p[:, 2 * GLA_KEY + 2 * MIX:]
        z = _dot(lora, aup_ref[...]) + ab_ref[...]
        yield
        gk = -_softplus(-z) / GLA_GATE_NORM
        br = _sel_dot(cs_ref[...], gk)
        yield
        bc = br[0:t]
        st = st_ref[b]
        o_inter = _dot_nt(q * jnp.exp(bc), st)
        kd = k * jnp.exp(br[(GLA_LEVELS) * t:(GLA_LEVELS + 1) * t])
        st_ref[b] = st * jnp.exp(bc[t - 1:t, :]) + _dot_tn(v, kd) * bd_ref[...]
        attn = [None] * GLA_HEADS
        for lvl in range(GLA_LEVELS):
            if lvl == 0:
                qt, kt = q, k
            else:
                e = jnp.exp(-jnp.abs(br[lvl * t:(lvl + 1) * t]))
                upper = (row & (2 ** (lvl - 1))) != 0
                qt = jnp.where(upper, q * e, 0.0)
                kt = jnp.where(upper, 0.0, k * e)
            pr = _dot_nt(_stack_heads(qt, hmk_ref, GLA_HEADS), kt)
            yield
            m = mask_ref[lvl]
            for h in range(GLA_HEADS):
                term = pr[h * t:(h + 1) * t] * m
                attn[h] = term if attn[h] is None else attn[h] + term
        ov = _dot(jnp.concatenate(attn, axis=0), v)
        yield
        o = o_inter + ov[0:t] * hmv_ref[0:1, :]
        for h in range(1, GLA_HEADS):
            o = o + ov[h * t:(h + 1) * t] * hmv_ref[h:h + 1, :]
        ms = _seg_dot(o * o, seg_ref[...])
        yield
        o = o * lax.rsqrt(ms + NORM_EPS)
        o_ref[b] = o * nw_ref[...] * (g * jax.nn.sigmoid(g))

    return [sequence(b) for b in range(p_ref.shape[0])]


def _gla_layer_consts(alpha_up, alpha_b, norm_w):
    lora_tile = GLA_WIDTH_PAD - 2 * GLA_KEY - 2 * MIX
    aup = jnp.pad(alpha_up, ((0, lora_tile - GLA_GATE_LORA), (0, 0))).astype(BF16)
    return aup, alpha_b.reshape(1, GLA_KEY), norm_w.reshape(1, MIX)


def _rwkv_consts():
    c, nh, hd = RWKV_C, RWKV_HEADS, RWKV_HEAD
    hm = np.zeros((nh, MIX), np.float32)
    for h in range(nh):
        hm[h, h * hd:(h + 1) * hd] = 1.0
    tri = np.tril(np.ones((c, c), np.float32))
    eye = np.eye(c, dtype=np.float32)
    incl = np.tile(tri, (1, nh))
    strict = np.tile(tri - eye, (1, nh))
    ident = np.tile(eye, (1, nh))
    seg = hm.T @ hm
    return (jnp.asarray(hm), jnp.asarray(tri, BF16), jnp.asarray(strict), jnp.asarray(incl),
            jnp.asarray(ident), jnp.asarray(seg, BF16), jnp.asarray(seg))


def _rwkv_streams(p_ref, mu_ref, w0_ref, wup_ref, a0_ref, aup_ref, gup_ref, kk_ref, ka_ref,
                  rk_ref, lnw_ref, lnb_ref, hm_ref, tri_ref, strict_ref, incl_ref, ident_ref,
                  seg_ref, bd_ref, o_ref, prev_ref, mt_ref):
    c, nh = RWKV_C, RWKV_HEADS
    n = nh * c
    row = lax.broadcasted_iota(jnp.int32, (c, RWKV_WIDTH), 0)

    def sequence(b):
        for r0 in range(0, p_ref.shape[1], c):
            yield from chunk(b, r0)

    def chunk(b, r0):
        p0 = p_ref[b, r0:r0 + c, :]
        shifted = jnp.where(row == 0, prev_ref[b], pltpu.roll(p0, 1, axis=0))
        prev_ref[b] = p0[c - 1:c, :]
        p = p0 + (shifted - p0) * mu_ref[...]
        r = p[:, 0:MIX]
        k = p[:, MIX:2 * MIX]
        v = p[:, 2 * MIX:3 * MIX]
        lora = p[:, 3 * MIX:]
        w = -_softplus(-(w0_ref[...] + _dot(jnp.tanh(lora), wup_ref[...]))) - 0.5
        wlog = -jnp.exp(w)
        a = jax.nn.sigmoid(a0_ref[...] + _dot(lora, aup_ref[...]))
        g = _dot(jax.nn.sigmoid(lora), gup_ref[...])
        yield
        kk = k * kk_ref[...]
        k = k * (1.0 + (a - 1.0) * ka_ref[...])
        sums = _seg_dot(jnp.concatenate([kk * kk, r * k * rk_ref[...]], axis=0), seg_ref[...])
        cl = _sel_dot(tri_ref[...], wlog)
        yield
        kk = kk / jnp.maximum(jnp.sqrt(sums[0:c]), 1e-12)
        bonus = sums[c:2 * c] * v
        bvec = kk * a
        g_inv = jnp.exp(-cl)
        at = -kk * jnp.exp(cl - wlog)
        rt = r * jnp.exp(cl)
        x = jnp.concatenate([at, rt], axis=0)
        y = jnp.concatenate([_stack_heads(bvec * g_inv, hm_ref, nh),
                             _stack_heads(k * g_inv, hm_ref, nh)], axis=0)
        pr = _dot_nt(x, y)
        mt = mt_ref[b]
        am = _dot_nt(x, mt)
        yield
        strict = strict_ref[...]
        incl = incl_ref[...]
        nab = pr[0:c, 0:n] * strict
        aak = pr[0:c, n:2 * n] * strict
        arb = pr[c:2 * c, 0:n] * incl
        ark = pr[c:2 * c, n:2 * n] * incl
        vbd = _stack_heads(v, hm_ref, nh)
        rhs = _dot(aak, vbd) + am[0:c]
        tm = ident_ref[...] + nab
        npow = nab
        nbd = _stack_heads(npow, hm_ref, nh)
        for _ in range(int(math.log2(c)) - 1):
            npow = _dot(npow, nbd)
            yield
            nbd = _stack_heads(npow, hm_ref, nh)
            tm = tm + _dot(tm, nbd)
            yield
        u = _dot(tm, _stack_heads(rhs, hm_ref, nh))
        yield
        o = am[c:2 * c] + _dot(jnp.concatenate([arb, ark], axis=1),
                               jnp.concatenate([_stack_heads(u, hm_ref, nh), vbd], axis=0))
        cl_last = cl[c - 1:c, :]
        tail = jnp.exp(cl_last - cl)
        upd = _dot_tn(jnp.concatenate([u, v], axis=0),
                      jnp.concatenate([bvec * tail, k * tail], axis=0))
        yield
        mt_ref[b] = mt * jnp.exp(cl_last) + upd * bd_ref[...]
        inv_n = 1.0 / RWKV_HEAD
        mean = _seg_dot(o, seg_ref[...]) * inv_n
        yield
        oc = o - mean
        var = _seg_dot(oc * oc, seg_ref[...]) * inv_n
        yield
        on = oc * lax.rsqrt(var + RWKV_GN_EPS) * lnw_ref[...] + lnb_ref[...]
        o_ref[b, r0:r0 + c, :] = (on + bonus) * g

    return [sequence(b) for b in range(p_ref.shape[0])]


def _rwkv_layer_consts(mu, w0, w_up, a0, a_up, g_up, k_k, k_a, r_k, ln_w, ln_b):
    row = lambda t: t.reshape(1, -1)
    place = lambda w, lo: jnp.pad(w, ((lo, 128 - lo - w.shape[0]), (0, 0))).astype(BF16)
    return (row(mu), row(w0), place(w_up, 0), row(a0), place(a_up, 32), place(g_up, 64), row(k_k),
            row(k_a), row(r_k), row(ln_w), row(ln_b))


def _mixers_kernel(*refs, counts):
    n_s5, n_rwkv, n_gla, n_rg = counts
    it = iter(refs)
    take = lambda k: [next(it) for _ in range(k)]
    u_ref, p_rwkv_ref, p_gla_ref, p_rg_ref = take(4)
    c_s5, c_rwkv, c_gla, c_rg = take(n_s5), take(n_rwkv), take(n_gla), take(n_rg)
    ya_ref, yb_ref, yc_ref, yd_ref = take(4)
    s_s5, s_rwkv, s_gla, s_rg = take(2), take(2), take(1), take(2)
    state_refs = [s_s5[0]] + s_rwkv + s_gla + s_rg

    @pl.when(pl.program_id(0) == 0)
    def _():
        for ref in state_refs:
            ref[...] = jnp.zeros_like(ref)

    rwkv = _rwkv_streams(p_rwkv_ref, *c_rwkv, yb_ref, *s_rwkv)
    gla = _gla_streams(p_gla_ref, *c_gla, yc_ref, *s_gla)
    rg = _rg_streams(p_rg_ref, *c_rg, yd_ref, *s_rg)
    s5 = [_s5_stream(u_ref, *c_s5, ya_ref, *s_s5)]
    gens = s5 + rwkv[:2] + rg[:2] + gla[:2] + rwkv[2:] + rg[2:] + gla[2:]
    _round_robin(gens)


def _mixer_layer_consts(bsz, s5_args, rwkv_args, gla_args, rg_args):
    bmat, pw_re, pw_im, *rest = jax.vmap(functools.partial(_s5_layer_consts, bsz))(*s5_args)
    pw_re = jnp.repeat(pw_re, SUB, axis=1)
    pw_im = jnp.repeat(pw_im, SUB, axis=1)
    s5 = (bmat, pw_re[:, 0:SUB], pw_im[:, 0:SUB], pw_re, pw_im, *rest)
    return (s5,
            jax.vmap(_rwkv_layer_consts)(*rwkv_args),
            jax.vmap(_gla_layer_consts)(*gla_args),
            jax.vmap(_rg_layer_consts)(*rg_args))


def _mixers(l, acts, layer_consts):
    bsz, seq, _ = acts[0].shape
    t = MIX_T
    shared = (_s5_shared(bsz), list(_rwkv_consts()), list(_gla_consts()), [])
    specs, operands, counts = [], [], []
    for lay, sh in zip(layer_consts, shared):
        specs += [_layer_spec(c, l) for c in lay] + [_const_spec(c.shape) for c in sh]
        operands += list(lay) + sh
        counts.append(len(lay) + len(sh))
    scratch = [pltpu.VMEM((2, SUB, S5_NSTATE), F32), pltpu.VMEM((bsz * t, 2 * S5_NSTATE), F32),
               pltpu.VMEM((bsz, 1, RWKV_WIDTH), F32), pltpu.VMEM((bsz, MIX, MIX), F32),
               pltpu.VMEM((bsz, MIX, GLA_KEY), F32),
               pltpu.VMEM((bsz, HALO + RG_T, MIX), F32), pltpu.VMEM((bsz, 1, MIX), F32)]
    seq_spec = lambda w: pl.BlockSpec((bsz, t, w), lambda i: (0, i, 0))
    return pl.pallas_call(
        functools.partial(_mixers_kernel, counts=tuple(counts)),
        grid=(seq // t,),
        in_specs=[seq_spec(a.shape[-1]) for a in acts] + specs,
        out_specs=[seq_spec(MIX)] * 4,
        out_shape=[jax.ShapeDtypeStruct((bsz, seq, MIX), F32)] * 4,
        scratch_shapes=scratch,
        compiler_params=_params(("arbitrary",)),
        name="mixers",
    )(*acts, *operands)


def _merge_kernel(x_ref, ya_ref, yb_ref, yc_ref, yd_ref, npre_ref, wg_ref, wb_ref, wo_ref,
                  npost_ref, o_ref):
    x = x_ref[...]
    h = _rms(x, npre_ref[...]).astype(BF16)
    merged = None
    for kbr, y_ref in enumerate((ya_ref, yb_ref, yc_ref, yd_ref)):
        gate = _sigmoid(jnp.dot(h, wg_ref[:, kbr * D_MODEL:(kbr + 1) * D_MODEL],
                                preferred_element_type=F32))
        term = gate * _dot(y_ref[...], wb_ref[kbr])
        merged = term if merged is None else merged + term
    m = _dot(merged, wo_ref[...])
    o_ref[...] = x + _rms(m, npost_ref[...])


def _merge(l, x2, ys, npre, wg, wb, wo, npost):
    n_tok = x2.shape[0]
    t = MERGE_T
    return pl.pallas_call(
        _merge_kernel,
        grid=(n_tok // t,),
        in_specs=[pl.BlockSpec((t, D_MODEL), lambda i: (i, 0))]
        + [pl.BlockSpec((t, MIX), lambda i: (i, 0))] * 4
        + [_layer_spec(a, l) for a in (npre, wg, wb, wo, npost)],
        out_specs=pl.BlockSpec((t, D_MODEL), lambda i: (i, 0)),
        out_shape=jax.ShapeDtypeStruct((n_tok, D_MODEL), F32),
        compiler_params=_params(("parallel",)),
        name="merge",
    )(x2, *ys, npre, wg, wb, wo, npost)


def _ffn_kernel(x_ref, npre_ref, win_ref, cw_ref, cb_ref, wout_ref, npost_ref, o_ref, gbuf_ref,
                *, tiles_per_seq):
    @pl.when(pl.program_id(0) % tiles_per_seq == 0)
    def _():
        gbuf_ref[0:HALO, :] = jnp.zeros((HALO, FFN_DIM), F32)

    t = FFN_T
    x = x_ref[...]
    h = _rms(x, npre_ref[...]).astype(BF16)
    gate = jnp.dot(h, win_ref[:, :FFN_DIM], preferred_element_type=F32)
    val = jnp.dot(h, win_ref[:, FFN_DIM:], preferred_element_type=F32)
    gbuf_ref[HALO:HALO + t, :] = gate
    gc = cb_ref[...] + cw_ref[FFN_CONV - 1:FFN_CONV, :] * gate
    for j in range(1, FFN_CONV):
        gc = gc + cw_ref[FFN_CONV - 1 - j:FFN_CONV - j, :] * gbuf_ref[HALO - j:HALO - j + t, :]
    gbuf_ref[0:HALO, :] = gate[t - HALO:t, :]
    f = _dot(jax.nn.gelu(gc) * val, wout_ref[...])
    o_ref[...] = x + _rms(f, npost_ref[...])


def _ffn(l, x2, seq, npre, w_in, conv_w, conv_b, w_out, npost):
    n_tok = x2.shape[0]
    t = FFN_T
    return pl.pallas_call(
        functools.partial(_ffn_kernel, tiles_per_seq=seq // t),
        grid=(n_tok // t,),
        in_specs=[pl.BlockSpec((t, D_MODEL), lambda i: (i, 0))]
        + [_layer_spec(a, l) for a in (npre, w_in, conv_w, conv_b, w_out, npost)],
        out_specs=pl.BlockSpec((t, D_MODEL), lambda i: (i, 0)),
        out_shape=jax.ShapeDtypeStruct((n_tok, D_MODEL), F32),
        scratch_shapes=[pltpu.VMEM((HALO + t, FFN_DIM), F32)],
        compiler_params=_params(("arbitrary",)),
        name="convffn",
    )(x2, npre, w_in, conv_w, conv_b, w_out, npost)


def kernel(x, norm_mix_pre, norm_mix_post, norm_ffn_pre, norm_ffn_post, w_in, s5_lambda_re, s5_lambda_im, s5_log_dt, s5_b_re, s5_b_im, s5_c_re, s5_c_im, s5_d, s5_w_glu, s5_b_glu, rwkv_mu, rwkv_w0, rwkv_w_up, rwkv_a0, rwkv_a_up, rwkv_g_up, rwkv_k_k, rwkv_k_a, rwkv_r_k, rwkv_ln_w, rwkv_ln_b, gla_alpha_up, gla_alpha_b, gla_norm_w, rg_conv_w, rg_conv_b, rg_w_a, rg_b_a, rg_w_x, rg_b_x, rg_lambda, w_branch, w_out, ffn_w_in, ffn_conv_w, ffn_conv_b, ffn_w_out):
    bsz, seq, _ = x.shape
    depth = w_in.shape[0]
    n_tok = bsz * seq
    x2 = x.reshape(n_tok, D_MODEL)
    o_rwkv = MIX
    o_gla = o_rwkv + 3 * MIX + 128
    o_rg = o_gla + GLA_WIDTH
    o_gate = o_rg + 2 * MIX
    w_mix = [w_in[:, :, :o_rwkv].astype(BF16), w_in[:, :, o_rwkv:o_gla].astype(BF16),
             lax.pad(w_in[:, :, o_gla:o_rg], jnp.zeros((), w_in.dtype),
                     [(0, 0, 0), (0, 0, 0), (0, GLA_WIDTH_PAD - GLA_WIDTH, 0)]).astype(BF16),
             w_in[:, :, o_rg:o_gate].astype(BF16)]
    w_gate = w_in[:, :, o_gate:].astype(BF16)
    w_branch_b, w_out_b = w_branch.astype(BF16), w_out.astype(BF16)
    ffn_w_in_b, ffn_w_out_b = ffn_w_in.astype(BF16), ffn_w_out.astype(BF16)
    vec = lambda t: t.reshape(depth, 1, t.shape[-1])
    npre, npost, fpre, fpost = (vec(norm_mix_pre), vec(norm_mix_post), vec(norm_ffn_pre),
                                vec(norm_ffn_post))
    mixer_consts = _mixer_layer_consts(
        bsz,
        (s5_lambda_re, s5_lambda_im, s5_log_dt, s5_b_re, s5_b_im, s5_c_re, s5_c_im, s5_d, s5_w_glu,
         s5_b_glu),
        (rwkv_mu, rwkv_w0, rwkv_w_up, rwkv_a0, rwkv_a_up, rwkv_g_up, rwkv_k_k, rwkv_k_a, rwkv_r_k,
         rwkv_ln_w, rwkv_ln_b),
        (gla_alpha_up, gla_alpha_b, gla_norm_w),
        (rg_conv_w, rg_conv_b, rg_w_a, rg_b_a, rg_w_x, rg_b_x, rg_lambda))
    for l in range(depth):
        ps = _proj(l, x2, npre, w_mix)
        ys = _mixers(l, [t.reshape(bsz, seq, t.shape[-1]) for t in ps], mixer_consts)
        ys = [t.reshape(n_tok, MIX) for t in ys]
        x2 = _merge(l, x2, ys, npre, w_gate, w_branch_b, w_out_b, npost)
        x2 = _ffn(l, x2, seq, fpre, ffn_w_in_b, ffn_conv_w, vec(ffn_conv_b), ffn_w_out_b, fpost)
    return x2.reshape(bsz, seq, D_MODEL)
```

```python
import functools
import math

import numpy as np
import jax
import jax.numpy as jnp
from jax import lax
from jax.experimental import pallas as pl
from jax.experimental.pallas import tpu as pltpu

F32 = jnp.float32
BF16 = jnp.bfloat16

D_MODEL = 1024
MIX = 256
NORM_EPS = 1e-6
S5_GROUPS, S5_GROUP, S5_STATE = 16, 16, 64
S5_NSTATE = S5_GROUPS * S5_STATE
S5_DT_MIN, S5_DT_MAX = 1e-3, 1e-1
RWKV_HEADS, RWKV_HEAD = 4, 64
RWKV_WIDTH = 3 * MIX + 128
RWKV_GN_EPS = 64e-5
GLA_HEADS, GLA_DK, GLA_DV = 4, 32, 64
GLA_KEY = GLA_HEADS * GLA_DK
GLA_GATE_LORA = 16
GLA_GATE_NORM = 16.0
GLA_WIDTH = 2 * GLA_KEY + 2 * MIX + GLA_GATE_LORA
GLA_WIDTH_PAD = 896
RG_CONV = 4
RG_C = 8.0
FFN_DIM = 2816
FFN_CONV = 3
HALO = 8

MIX_T = 128
S5_T = RG_T = GLA_T = MIX_T
RWKV_C = 64
SUB = 8
PROJ_T = 1024
MERGE_T = 512
FFN_T = 512

VMEM_LIMIT = 56 * 1024 * 1024


def _dot(a, b):
    return jnp.dot(a.astype(BF16), b.astype(BF16), preferred_element_type=F32)


def _dot_nt(a, b):
    return lax.dot_general(a.astype(BF16), b.astype(BF16), (((1,), (1,)), ((), ())),
                           preferred_element_type=F32)


def _dot_tn(a, b):
    return lax.dot_general(a.astype(BF16), b.astype(BF16), (((0,), (0,)), ((), ())),
                           preferred_element_type=F32)


def _split3(x):
    hi = x.astype(BF16)
    r1 = x - hi.astype(F32)
    mid = r1.astype(BF16)
    lo = (r1 - mid.astype(F32)).astype(BF16)
    return hi, mid, lo


def _sel_dot(c, x):
    hi, mid, lo = _split3(x)
    return (jnp.dot(c, hi, preferred_element_type=F32)
            + jnp.dot(c, mid, preferred_element_type=F32)
            + jnp.dot(c, lo, preferred_element_type=F32))


def _seg_dot(x, j):
    hi = x.astype(BF16)
    lo = (x - hi.astype(F32)).astype(BF16)
    m = x.shape[0]
    r = jnp.dot(jnp.concatenate([hi, lo], axis=0), j, preferred_element_type=F32)
    return r[0:m] + r[m:2 * m]


def _seg_dot_left(c, x):
    hi = x.astype(BF16)
    lo = (x - hi.astype(F32)).astype(BF16)
    n = x.shape[1]
    r = jnp.dot(c, jnp.concatenate([hi, lo], axis=1), preferred_element_type=F32)
    return r[:, 0:n] + r[:, n:2 * n]


def _rms(x, w):
    return x * lax.rsqrt(jnp.mean(x * x, axis=-1, keepdims=True) + NORM_EPS) * w


def _softplus(y):
    return jnp.maximum(y, 0.0) + jnp.log1p(jnp.exp(-jnp.abs(y)))


def _shift_rows(x, s, fill):
    rolled = pltpu.roll(x, s, axis=0)
    row = lax.broadcasted_iota(jnp.int32, x.shape, 0)
    return jnp.where(row >= s, rolled, fill)


def _stack_heads(x, hm_ref, nheads):
    xb = x.astype(BF16)
    return jnp.concatenate([xb * hm_ref[h:h + 1, :].astype(BF16) for h in range(nheads)], axis=0)


def _round_robin(gens):
    while gens:
        alive = []
        for gen in gens:
            try:
                next(gen)
                alive.append(gen)
            except StopIteration:
                pass
        gens = alive


def _const_spec(shape):
    nd = len(shape)
    return pl.BlockSpec(shape, lambda *_: (0,) * nd, pipeline_mode=pl.Buffered(1))


def _layer_spec(arr, l):
    nd = arr.ndim
    return pl.BlockSpec((None,) + arr.shape[1:], lambda *_: (l,) + (0,) * (nd - 1),
                        pipeline_mode=pl.Buffered(1))


def _sigmoid(x):
    return 0.5 * jnp.tanh(0.5 * x) + 0.5


def _params(sem):
    return pltpu.CompilerParams(dimension_semantics=sem, vmem_limit_bytes=VMEM_LIMIT)


def _proj_kernel(x_ref, nw_ref, *refs):
    n = len(refs) // 2
    h = _rms(x_ref[...], nw_ref[...]).astype(BF16)
    for w_ref, o_ref in zip(refs[:n], refs[n:]):
        o_ref[...] = jnp.dot(h, w_ref[...], preferred_element_type=F32)


def _proj(l, x2, nw, ws):
    n_tok = x2.shape[0]
    widths = [w.shape[-1] for w in ws]
    return pl.pallas_call(
        _proj_kernel,
        grid=(n_tok // PROJ_T,),
        in_specs=[pl.BlockSpec((PROJ_T, D_MODEL), lambda i: (i, 0)), _layer_spec(nw, l)]
        + [_layer_spec(w, l) for w in ws],
        out_specs=[pl.BlockSpec((PROJ_T, wd), lambda i: (i, 0)) for wd in widths],
        out_shape=[jax.ShapeDtypeStruct((n_tok, wd), F32) for wd in widths],
        compiler_params=_params(("parallel",)),
        name="proj",
    )(x2, nw, *ws)


def _cmul_add(hre, him, ar, ai, sre, sim):
    return hre + ar * sre - ai * sim, him + ar * sim + ai * sre


def _block_ends(x, nblk):
    return jnp.concatenate([x[j * SUB + SUB - 1:(j + 1) * SUB] for j in range(nblk)], axis=0)


S5_YIELD_EVERY = 8


def _s5_stream(u_ref, bmat_ref, are_ref, aim_ref, pwre_ref, pwim_ref, cmat_ref, d_ref, wglu_ref,
               bglu_ref, perm_ref, permt_ref, o_ref, h0_ref, hbuf_ref):
    bsz, t = u_ref.shape[0], S5_T
    nseg = SUB // bsz
    seglen = t // nseg
    rows = bsz * t
    ns = S5_NSTATE
    u = u_ref[...].reshape(rows, MIX)
    up = jnp.dot(perm_ref[...], u.astype(BF16), preferred_element_type=F32).astype(BF16)
    hbuf_ref[...] = jnp.dot(up, bmat_ref[...], preferred_element_type=F32)
    yield
    are = are_ref[...]
    aim = aim_ref[...]
    hre = h0_ref[0]
    him = h0_ref[1]
    for tt in range(seglen):
        r0 = tt * SUB
        hre, him = _cmul_add(hbuf_ref[r0:r0 + SUB, 0:ns], hbuf_ref[r0:r0 + SUB, ns:2 * ns],
                             are, aim, hre, him)
        hbuf_ref[r0:r0 + SUB, 0:ns] = hre
        hbuf_ref[r0:r0 + SUB, ns:2 * ns] = him
        if tt % S5_YIELD_EVERY == S5_YIELD_EVERY - 1:
            yield
    seg = lax.broadcasted_iota(jnp.int32, (SUB, ns), 0) & (nseg - 1)
    later = seg >= 1
    pre = pwre_ref[rows - SUB:rows, :]
    pim = pwim_ref[rows - SUB:rows, :]
    tre, tim = hre, him
    for _ in range(nseg - 1):
        sre = jnp.where(later, pltpu.roll(tre, 1, axis=0), 0.0)
        sim = jnp.where(later, pltpu.roll(tim, 1, axis=0), 0.0)
        tre, tim = _cmul_add(hre, him, pre, pim, sre, sim)
    cre = jnp.where(later, pltpu.roll(tre, 1, axis=0), 0.0)
    cim = jnp.where(later, pltpu.roll(tim, 1, axis=0), 0.0)
    first = seg == 0
    h0_ref[0] = jnp.where(first, pltpu.roll(tre, SUB - (nseg - 1), axis=0), 0.0)
    h0_ref[1] = jnp.where(first, pltpu.roll(tim, SUB - (nseg - 1), axis=0), 0.0)
    for tt in range(seglen):
        r0 = tt * SUB
        fre, fim = _cmul_add(hbuf_ref[r0:r0 + SUB, 0:ns], hbuf_ref[r0:r0 + SUB, ns:2 * ns],
                             pwre_ref[r0:r0 + SUB, :], pwim_ref[r0:r0 + SUB, :], cre, cim)
        hbuf_ref[r0:r0 + SUB, 0:ns] = fre
        hbuf_ref[r0:r0 + SUB, ns:2 * ns] = fim
        if tt % S5_YIELD_EVERY == S5_YIELD_EVERY - 1:
            yield
    yp = _dot(hbuf_ref[...], cmat_ref[...])
    yield
    y = _seg_dot_left(permt_ref[...], yp) + d_ref[...] * u
    yield
    y = jax.nn.gelu(y)
    y = y * jax.nn.sigmoid(_dot(y, wglu_ref[...]) + bglu_ref[...])
    o_ref[...] = y.reshape(bsz, t, MIX)


def _s5_shared(bsz):
    rows = bsz * S5_T
    seglen = rows // SUB
    src = (np.arange(rows) % SUB) * seglen + np.arange(rows) // SUB
    perm = np.zeros((rows, rows), np.float32)
    perm[np.arange(rows), src] = 1.0
    return [jnp.asarray(perm, BF16), jnp.asarray(perm.T, BF16)]


def _s5_layer_consts(bsz, lam_re, lam_im, log_dt, b_re, b_im, c_re, c_im, d, w_glu, b_glu):
    dt = jnp.exp(log_dt).reshape(S5_GROUPS, 1)
    mag = jnp.exp(lam_re * dt)
    ang = lam_im * dt
    ab_re = mag * jnp.cos(ang)
    ab_im = mag * jnp.sin(ang)
    den = lam_re * lam_re + lam_im * lam_im
    f_re = ((ab_re - 1.0) * lam_re + ab_im * lam_im) / den
    f_im = (ab_im * lam_re - (ab_re - 1.0) * lam_im) / den
    f_re = f_re.reshape(S5_GROUPS, S5_STATE, 1)
    f_im = f_im.reshape(S5_GROUPS, S5_STATE, 1)
    bb_re = f_re * b_re - f_im * b_im
    bb_im = f_re * b_im + f_im * b_re
    eye = jnp.eye(S5_GROUPS, dtype=F32)
    bm_re = jnp.einsum('gpc,gh->gchp', bb_re, eye).reshape(MIX, S5_NSTATE)
    bm_im = jnp.einsum('gpc,gh->gchp', bb_im, eye).reshape(MIX, S5_NSTATE)
    bmat = jnp.concatenate([bm_re, bm_im], axis=1).astype(BF16)
    cm_re = jnp.einsum('gcp,gh->gphc', c_re, eye).reshape(S5_NSTATE, MIX)
    cm_im = jnp.einsum('gcp,gh->gphc', c_im, eye).reshape(S5_NSTATE, MIX)
    cmat = jnp.concatenate([cm_re, -cm_im], axis=0).astype(BF16)
    assert SUB % bsz == 0
    seglen = S5_T * bsz // SUB
    pw_re = ab_re.reshape(1, S5_NSTATE)
    pw_im = ab_im.reshape(1, S5_NSTATE)
    while pw_re.shape[0] < seglen:
        top_re, top_im = pw_re[-1:], pw_im[-1:]
        pw_re, pw_im = (jnp.concatenate([pw_re, pw_re * top_re - pw_im * top_im], axis=0),
                        jnp.concatenate([pw_im, pw_re * top_im + pw_im * top_re], axis=0))
    return (bmat, pw_re, pw_im, cmat, d.reshape(1, MIX), w_glu.astype(BF16), b_glu.reshape(1, MIX))


def _rg_streams(p_ref, cw_ref, cb_ref, wa_ref, ba_ref, wx_ref, bx_ref, sp_ref,
                o_ref, xbuf_ref, h_ref):
    first = pl.program_id(0) == 0
    t, nblk = RG_T, RG_T // SUB
    row = lax.broadcasted_iota(jnp.int32, (t, MIX), 0)
    sub_row = lax.broadcasted_iota(jnp.int32, (nblk, SUB, MIX), 1)

    def sequence(bi):
        p = p_ref[bi]
        x = p[:, :MIX]
        y = p[:, MIX:]
        xbuf_ref[bi, HALO:HALO + t, :] = x
        xc = cb_ref[...] + cw_ref[RG_CONV - 1:RG_CONV, :] * x
        for j in range(1, RG_CONV):
            xc = xc + (cw_ref[RG_CONV - 1 - j:RG_CONV - j, :]
                       * xbuf_ref[bi, HALO - j:HALO - j + t, :])
        xbuf_ref[bi, 0:HALO, :] = x[t - HALO:t, :]
        za = _dot(xc, wa_ref[...])
        zx = _dot(xc, wx_ref[...])
        yield
        gate_r = jax.nn.sigmoid(za + ba_ref[...])
        gate_i = jax.nn.sigmoid(zx + bx_ref[...])
        log_a = -RG_C * gate_r * sp_ref[...]
        a = jnp.exp(log_a)
        th = jnp.tanh(log_a)
        mult = jnp.sqrt(-2.0 * th / (1.0 - th))
        mult = jnp.where(jnp.logical_and(first, row == 0), 1.0, mult)
        b = mult * (gate_i * xc)
        a = a.reshape(nblk, SUB, MIX)
        b = b.reshape(nblk, SUB, MIX)
        for i in range(int(math.log2(SUB))):
            live = sub_row >= 2 ** i
            b = b + a * jnp.where(live, pltpu.roll(b, 2 ** i, axis=1), 0.0)
            a = a * jnp.where(live, pltpu.roll(a, 2 ** i, axis=1), 1.0)
        a = a.reshape(t, MIX)
        b = b.reshape(t, MIX)
        yield
        ea = _block_ends(a, nblk)
        eb = _block_ends(b, nblk)
        s = 1
        while s < nblk:
            eb = eb + ea * _shift_rows(eb, s, 0.0)
            ea = ea * _shift_rows(ea, s, 1.0)
            s *= 2
        h_in = h_ref[bi]
        c = _shift_rows(eb, 1, 0.0) + _shift_rows(ea, 1, 1.0) * h_in
        h_ref[bi] = eb[nblk - 1:nblk] + ea[nblk - 1:nblk] * h_in
        hs = [b[j * SUB:(j + 1) * SUB] + a[j * SUB:(j + 1) * SUB] * c[j:j + 1] for j in range(nblk)]
        o_ref[bi] = jnp.concatenate(hs, axis=0) * jax.nn.gelu(y)

    return [sequence(bi) for bi in range(p_ref.shape[0])]


def _block_diag(w):
    nh, n, _ = w.shape
    eye = jnp.eye(nh, dtype=w.dtype)
    return jnp.einsum('hij,hg->higj', w, eye).reshape(nh * n, nh * n)


def _rg_layer_consts(conv_w, conv_b, w_a, b_a, w_x, b_x, lam):
    sp = jax.nn.softplus(-lam).reshape(1, MIX)
    row = lambda t: t.reshape(1, MIX)
    return (conv_w, row(conv_b), _block_diag(w_a).astype(BF16), row(b_a),
            _block_diag(w_x).astype(BF16), row(b_x), sp)


GLA_LEVELS = int(math.log2(GLA_T)) + 1


def _gla_consts():
    t = GLA_T
    tri = np.tril(np.ones((t, t), np.float32))
    cs = [tri]
    masks = [np.eye(t, dtype=np.float32)]
    idx = np.arange(t)
    for lvl in range(1, GLA_LEVELS):
        s = 2 ** (lvl - 1)
        mid = (idx // (2 * s)) * (2 * s) + s
        cs.append(tri[mid - 1] - tri)
        masks.append(((idx[:, None] ^ idx[None, :]) < 2 * s).astype(np.float32))
    cs.append(tri[t - 1:t] - tri)
    hm_k = np.zeros((GLA_HEADS, GLA_KEY), np.float32)
    hm_v = np.zeros((GLA_HEADS, MIX), np.float32)
    for h in range(GLA_HEADS):
        hm_k[h, h * GLA_DK:(h + 1) * GLA_DK] = 1.0
        hm_v[h, h * GLA_DV:(h + 1) * GLA_DV] = 1.0
    bd = hm_v.T @ hm_k
    seg = (hm_v.T @ hm_v) / GLA_DV
    return (jnp.asarray(np.concatenate(cs, axis=0), BF16), jnp.asarray(np.stack(masks), F32),
            jnp.asarray(hm_k), jnp.asarray(hm_v), jnp.asarray(bd), jnp.asarray(seg, BF16))


def _gla_streams(p_ref, aup_ref, ab_ref, nw_ref, cs_ref, mask_ref, hmk_ref, hmv_ref, bd_ref,
                 seg_ref, o_ref, st_ref):
    t = GLA_T
    row = lax.broadcasted_iota(jnp.int32, (t, GLA_KEY), 0)

    def sequence(b):
        p = p_ref[b]
        q = p[:, 0:GLA_KEY] * (GLA_DK ** -0.5)
        k = p[:, GLA_KEY:2 * GLA_KEY]
        v = p[:, 2 * GLA_KEY:2 * GLA_KEY + MIX]
        g = p[:, 2 * GLA_KEY + MIX:2 * GLA_KEY + 2 * MIX]
        lora = p[:, 2 * GLA_KEY + 2 * MIX:]
        z = _dot(lora, aup_ref[...]) + ab_ref[...]
        yield
        gk = -_softplus(-z) / GLA_GATE_NORM
        br = _sel_dot(cs_ref[...], gk)
        yield
        bc = br[0:t]
        st = st_ref[b]
        o_inter = _dot_nt(q * jnp.exp(bc), st)
        kd = k * jnp.exp(br[(GLA_LEVELS) * t:(GLA_LEVELS + 1) * t])
        st_ref[b] = st * jnp.exp(bc[t - 1:t, :]) + _dot_tn(v, kd) * bd_ref[...]
        attn = [None] * GLA_HEADS
        for lvl in range(GLA_LEVELS):
            if lvl == 0:
                qt, kt = q, k
            else:
                e = jnp.exp(-jnp.abs(br[lvl * t:(lvl + 1) * t]))
                upper = (row & (2 ** (lvl - 1))) != 0
                qt = jnp.where(upper, q * e, 0.0)
                kt = jnp.where(upper, 0.0, k * e)
            pr = _dot_nt(_stack_heads(qt, hmk_ref, GLA_HEADS), kt)
            yield
            m = mask_ref[lvl]
            for h in range(GLA_HEADS):
                term = pr[h * t:(h + 1) * t] * m
                attn[h] = term if attn[h] is None else attn[h] + term
        ov = _dot(jnp.concatenate(attn, axis=0), v)
        yield
        o = o_inter + ov[0:t] * hmv_ref[0:1, :]
        for h in range(1, GLA_HEADS):
            o = o + ov[h * t:(h + 1) * t] * hmv_ref[h:h + 1, :]
        ms = _seg_dot(o * o, seg_ref[...])
        yield
        o = o * lax.rsqrt(ms + NORM_EPS)
        o_ref[b] = o * nw_ref[...] * (g * jax.nn.sigmoid(g))

    return [sequence(b) for b in range(p_ref.shape[0])]


def _gla_layer_consts(alpha_up, alpha_b, norm_w):
    lora_tile = GLA_WIDTH_PAD - 2 * GLA_KEY - 2 * MIX
    aup = jnp.pad(alpha_up, ((0, lora_tile - GLA_GATE_LORA), (0, 0))).astype(BF16)
    return aup, alpha_b.reshape(1, GLA_KEY), norm_w.reshape(1, MIX)


def _rwkv_consts():
    c, nh, hd = RWKV_C, RWKV_HEADS, RWKV_HEAD
    hm = np.zeros((nh, MIX), np.float32)
    for h in range(nh):
        hm[h, h * hd:(h + 1) * hd] = 1.0
    tri = np.tril(np.ones((c, c), np.float32))
    eye = np.eye(c, dtype=np.float32)
    incl = np.tile(tri, (1, nh))
    strict = np.tile(tri - eye, (1, nh))
    ident = np.tile(eye, (1, nh))
    seg = hm.T @ hm
    return (jnp.asarray(hm), jnp.asarray(tri, BF16), jnp.asarray(strict), jnp.asarray(incl),
            jnp.asarray(ident), jnp.asarray(seg, BF16), jnp.asarray(seg))


def _rwkv_streams(p_ref, mu_ref, w0_ref, wup_ref, a0_ref, aup_ref, gup_ref, kk_ref, ka_ref,
                  rk_ref, lnw_ref, lnb_ref, hm_ref, tri_ref, strict_ref, incl_ref, ident_ref,
                  seg_ref, bd_ref, o_ref, prev_ref, mt_ref):
    c, nh = RWKV_C, RWKV_HEADS
    n = nh * c
    row = lax.broadcasted_iota(jnp.int32, (c, RWKV_WIDTH), 0)

    def sequence(b):
        for r0 in range(0, p_ref.shape[1], c):
            yield from chunk(b, r0)

    def chunk(b, r0):
        p0 = p_ref[b, r0:r0 + c, :]
        shifted = jnp.where(row == 0, prev_ref[b], pltpu.roll(p0, 1, axis=0))
        prev_ref[b] = p0[c - 1:c, :]
        p = p0 + (shifted - p0) * mu_ref[...]
        r = p[:, 0:MIX]
        k = p[:, MIX:2 * MIX]
        v = p[:, 2 * MIX:3 * MIX]
        lora = p[:, 3 * MIX:]
        w = -_softplus(-(w0_ref[...] + _dot(jnp.tanh(lora), wup_ref[...]))) - 0.5
        wlog = -jnp.exp(w)
        a = jax.nn.sigmoid(a0_ref[...] + _dot(lora, aup_ref[...]))
        g = _dot(jax.nn.sigmoid(lora), gup_ref[...])
        yield
        kk = k * kk_ref[...]
        k = k * (1.0 + (a - 1.0) * ka_ref[...])
        sums = _seg_dot(jnp.concatenate([kk * kk, r * k * rk_ref[...]], axis=0), seg_ref[...])
        cl = _sel_dot(tri_ref[...], wlog)
        yield
        kk = kk / jnp.maximum(jnp.sqrt(sums[0:c]), 1e-12)
        bonus = sums[c:2 * c] * v
        bvec = kk * a
        g_inv = jnp.exp(-cl)
        at = -kk * jnp.exp(cl - wlog)
        rt = r * jnp.exp(cl)
        x = jnp.concatenate([at, rt], axis=0)
        y = jnp.concatenate([_stack_heads(bvec * g_inv, hm_ref, nh),
                             _stack_heads(k * g_inv, hm_ref, nh)], axis=0)
        pr = _dot_nt(x, y)
        mt = mt_ref[b]
        am = _dot_nt(x, mt)
        yield
        strict = strict_ref[...]
        incl = incl_ref[...]
        nab = pr[0:c, 0:n] * strict
        aak = pr[0:c, n:2 * n] * strict
        arb = pr[c:2 * c, 0:n] * incl
        ark = pr[c:2 * c, n:2 * n] * incl
        vbd = _stack_heads(v, hm_ref, nh)
        rhs = _dot(aak, vbd) + am[0:c]
        tm = ident_ref[...] + nab
        npow = nab
        nbd = _stack_heads(npow, hm_ref, nh)
        for _ in range(int(math.log2(c)) - 1):
            npow = _dot(npow, nbd)
            yield
            nbd = _stack_heads(npow, hm_ref, nh)
            tm = tm + _dot(tm, nbd)
            yield
        u = _dot(tm, _stack_heads(rhs, hm_ref, nh))
        yield
        o = am[c:2 * c] + _dot(jnp.concatenate([arb, ark], axis=1),
                               jnp.concatenate([_stack_heads(u, hm_ref, nh), vbd], axis=0))
        cl_last = cl[c - 1:c, :]
        tail = jnp.exp(cl_last - cl)
        upd = _dot_tn(jnp.concatenate([u, v], axis=0),
                      jnp.concatenate([bvec * tail, k * tail], axis=0))
        yield
        mt_ref[b] = mt * jnp.exp(cl_last) + upd * bd_ref[...]
        inv_n = 1.0 / RWKV_HEAD
        mean = _seg_dot(o, seg_ref[...]) * inv_n
        yield
        oc = o - mean
        var = _seg_dot(oc * oc, seg_ref[...]) * inv_n
        yield
        on = oc * lax.rsqrt(var + RWKV_GN_EPS) * lnw_ref[...] + lnb_ref[...]
        o_ref[b, r0:r0 + c, :] = (on + bonus) * g

    return [sequence(b) for b in range(p_ref.shape[0])]


def _rwkv_layer_consts(mu, w0, w_up, a0, a_up, g_up, k_k, k_a, r_k, ln_w, ln_b):
    row = lambda t: t.reshape(1, -1)
    place = lambda w, lo: jnp.pad(w, ((lo, 128 - lo - w.shape[0]), (0, 0))).astype(BF16)
    return (row(mu), row(w0), place(w_up, 0), row(a0), place(a_up, 32), place(g_up, 64), row(k_k),
            row(k_a), row(r_k), row(ln_w), row(ln_b))


def _mixers_kernel(*refs, counts):
    n_s5, n_rwkv, n_gla, n_rg = counts
    it = iter(refs)
    take = lambda k: [next(it) for _ in range(k)]
    u_ref, p_rwkv_ref, p_gla_ref, p_rg_ref = take(4)
    c_s5, c_rwkv, c_gla, c_rg = take(n_s5), take(n_rwkv), take(n_gla), take(n_rg)
    ya_ref, yb_ref, yc_ref, yd_ref = take(4)
    s_s5, s_rwkv, s_gla, s_rg = take(2), take(2), take(1), take(2)
    state_refs = [s_s5[0]] + s_rwkv + s_gla + s_rg

    @pl.when(pl.program_id(0) == 0)
    def _():
        for ref in state_refs:
            ref[...] = jnp.zeros_like(ref)

    rwkv = _rwkv_streams(p_rwkv_ref, *c_rwkv, yb_ref, *s_rwkv)
    gla = _gla_streams(p_gla_ref, *c_gla, yc_ref, *s_gla)
    rg = _rg_streams(p_rg_ref, *c_rg, yd_ref, *s_rg)
    s5 = [_s5_stream(u_ref, *c_s5, ya_ref, *s_s5)]
    gens = s5 + rwkv[:2] + rg[:2] + gla[:2] + rwkv[2:] + rg[2:] + gla[2:]
    _round_robin(gens)


def _mixer_layer_consts(bsz, s5_args, rwkv_args, gla_args, rg_args):
    bmat, pw_re, pw_im, *rest = jax.vmap(functools.partial(_s5_layer_consts, bsz))(*s5_args)
    pw_re = jnp.repeat(pw_re, SUB, axis=1)
    pw_im = jnp.repeat(pw_im, SUB, axis=1)
    s5 = (bmat, pw_re[:, 0:SUB], pw_im[:, 0:SUB], pw_re, pw_im, *rest)
    return (s5,
            jax.vmap(_rwkv_layer_consts)(*rwkv_args),
            jax.vmap(_gla_layer_consts)(*gla_args),
            jax.vmap(_rg_layer_consts)(*rg_args))


def _mixers(l, acts, layer_consts):
    bsz, seq, _ = acts[0].shape
    t = MIX_T
    shared = (_s5_shared(bsz), list(_rwkv_consts()), list(_gla_consts()), [])
    specs, operands, counts = [], [], []
    for lay, sh in zip(layer_consts, shared):
        specs += [_layer_spec(c, l) for c in lay] + [_const_spec(c.shape) for c in sh]
        operands += list(lay) + sh
        counts.append(len(lay) + len(sh))
    scratch = [pltpu.VMEM((2, SUB, S5_NSTATE), F32), pltpu.VMEM((bsz * t, 2 * S5_NSTATE), F32),
               pltpu.VMEM((bsz, 1, RWKV_WIDTH), F32), pltpu.VMEM((bsz, MIX, MIX), F32),
               pltpu.VMEM((bsz, MIX, GLA_KEY), F32),
               pltpu.VMEM((bsz, HALO + RG_T, MIX), F32), pltpu.VMEM((bsz, 1, MIX), F32)]
    seq_spec = lambda w: pl.BlockSpec((bsz, t, w), lambda i: (0, i, 0))
    return pl.pallas_call(
        functools.partial(_mixers_kernel, counts=tuple(counts)),
        grid=(seq // t,),
        in_specs=[seq_spec(a.shape[-1]) for a in acts] + specs,
        out_specs=[seq_spec(MIX)] * 4,
        out_shape=[jax.ShapeDtypeStruct((bsz, seq, MIX), F32)] * 4,
        scratch_shapes=scratch,
        compiler_params=_params(("arbitrary",)),
        name="mixers",
    )(*acts, *operands)


def _merge_kernel(x_ref, ya_ref, yb_ref, yc_ref, yd_ref, npre_ref, wg_ref, wb_ref, wo_ref,
                  npost_ref, o_ref):
    x = x_ref[...]
    h = _rms(x, npre_ref[...]).astype(BF16)
    merged = None
    for kbr, y_ref in enumerate((ya_ref, yb_ref, yc_ref, yd_ref)):
        gate = _sigmoid(jnp.dot(h, wg_ref[:, kbr * D_MODEL:(kbr + 1) * D_MODEL],
                                preferred_element_type=F32))
        term = gate * _dot(y_ref[...], wb_ref[kbr])
        merged = term if merged is None else merged + term
    m = _dot(merged, wo_ref[...])
    o_ref[...] = x + _rms(m, npost_ref[...])


def _merge(l, x2, ys, npre, wg, wb, wo, npost):
    n_tok = x2.shape[0]
    t = MERGE_T
    return pl.pallas_call(
        _merge_kernel,
        grid=(n_tok // t,),
        in_specs=[pl.BlockSpec((t, D_MODEL), lambda i: (i, 0))]
        + [pl.BlockSpec((t, MIX), lambda i: (i, 0))] * 4
        + [_layer_spec(a, l) for a in (npre, wg, wb, wo, npost)],
        out_specs=pl.BlockSpec((t, D_MODEL), lambda i: (i, 0)),
        out_shape=jax.ShapeDtypeStruct((n_tok, D_MODEL), F32),
        compiler_params=_params(("parallel",)),
        name="merge",
    )(x2, *ys, npre, wg, wb, wo, npost)


def _ffn_kernel(x_ref, npre_ref, win_ref, cw_ref, cb_ref, wout_ref, npost_ref, o_ref, gbuf_ref,
                *, tiles_per_seq):
    @pl.when(pl.program_id(0) % tiles_per_seq == 0)
    def _():
        gbuf_ref[0:HALO, :] = jnp.zeros((HALO, FFN_DIM), F32)

    t = FFN_T
    x = x_ref[...]
    h = _rms(x, npre_ref[...]).astype(BF16)
    gate = jnp.dot(h, win_ref[:, :FFN_DIM], preferred_element_type=F32)
    val = jnp.dot(h, win_ref[:, FFN_DIM:], preferred_element_type=F32)
    gbuf_ref[HALO:HALO + t, :] = gate
    gc = cb_ref[...] + cw_ref[FFN_CONV - 1:FFN_CONV, :] * gate
    for j in range(1, FFN_CONV):
        gc = gc + cw_ref[FFN_CONV - 1 - j:FFN_CONV - j, :] * gbuf_ref[HALO - j:HALO - j + t, :]
    gbuf_ref[0:HALO, :] = gate[t - HALO:t, :]
    f = _dot(jax.nn.gelu(gc) * val, wout_ref[...])
    o_ref[...] = x + _rms(f, npost_ref[...])


def _ffn(l, x2, seq, npre, w_in, conv_w, conv_b, w_out, npost):
    n_tok = x2.shape[0]
    t = FFN_T
    return pl.pallas_call(
        functools.partial(_ffn_kernel, tiles_per_seq=seq // t),
        grid=(n_tok // t,),
        in_specs=[pl.BlockSpec((t, D_MODEL), lambda i: (i, 0))]
        + [_layer_spec(a, l) for a in (npre, w_in, conv_w, conv_b, w_out, npost)],
        out_specs=pl.BlockSpec((t, D_MODEL), lambda i: (i, 0)),
        out_shape=jax.ShapeDtypeStruct((n_tok, D_MODEL), F32),
        scratch_shapes=[pltpu.VMEM((HALO + t, FFN_DIM), F32)],
        compiler_params=_params(("arbitrary",)),
        name="convffn",
    )(x2, npre, w_in, conv_w, conv_b, w_out, npost)


def kernel(x, norm_mix_pre, norm_mix_post, norm_ffn_pre, norm_ffn_post, w_in, s5_lambda_re, s5_lambda_im, s5_log_dt, s5_b_re, s5_b_im, s5_c_re, s5_c_im, s5_d, s5_w_glu, s5_b_glu, rwkv_mu, rwkv_w0, rwkv_w_up, rwkv_a0, rwkv_a_up, rwkv_g_up, rwkv_k_k, rwkv_k_a, rwkv_r_k, rwkv_ln_w, rwkv_ln_b, gla_alpha_up, gla_alpha_b, gla_norm_w, rg_conv_w, rg_conv_b, rg_w_a, rg_b_a, rg_w_x, rg_b_x, rg_lambda, w_branch, w_out, ffn_w_in, ffn_conv_w, ffn_conv_b, ffn_w_out):
    bsz, seq, _ = x.shape
    depth = w_in.shape[0]
    n_tok = bsz * seq
    x2 = x.reshape(n_tok, D_MODEL)
    o_rwkv = MIX
    o_gla = o_rwkv + 3 * MIX + 128
    o_rg = o_gla + GLA_WIDTH
    o_gate = o_rg + 2 * MIX
    w_in_b = w_in.astype(BF16)
    w_mix = [w_in_b[:, :, :o_rwkv], w_in_b[:, :, o_rwkv:o_gla],
             lax.pad(w_in_b[:, :, o_gla:o_rg], jnp.zeros((), BF16),
                     [(0, 0, 0), (0, 0, 0), (0, GLA_WIDTH_PAD - GLA_WIDTH, 0)]),
             w_in_b[:, :, o_rg:o_gate]]
    w_gate = w_in_b[:, :, o_gate:]
    w_branch_b, w_out_b = w_branch.astype(BF16), w_out.astype(BF16)
    ffn_w_in_b, ffn_w_out_b = ffn_w_in.astype(BF16), ffn_w_out.astype(BF16)
    vec = lambda t: t.reshape(depth, 1, t.shape[-1])
    npre, npost, fpre, fpost = (vec(norm_mix_pre), vec(norm_mix_post), vec(norm_ffn_pre),
                                vec(norm_ffn_post))
    mixer_consts = _mixer_layer_consts(
        bsz,
        (s5_lambda_re, s5_lambda_im, s5_log_dt, s5_b_re, s5_b_im, s5_c_re, s5_c_im, s5_d, s5_w_glu,
         s5_b_glu),
        (rwkv_mu, rwkv_w0, rwkv_w_up, rwkv_a0, rwkv_a_up, rwkv_g_up, rwkv_k_k, rwkv_k_a, rwkv_r_k,
         rwkv_ln_w, rwkv_ln_b),
        (gla_alpha_up, gla_alpha_b, gla_norm_w),
        (rg_conv_w, rg_conv_b, rg_w_a, rg_b_a, rg_w_x, rg_b_x, rg_lambda))
    for l in range(depth):
        ps = _proj(l, x2, npre, w_mix)
        ys = _mixers(l, [t.reshape(bsz, seq, t.shape[-1]) for t in ps], mixer_consts)
        ys = [t.reshape(n_tok, MIX) for t in ys]
        x2 = _merge(l, x2, ys, npre, w_gate, w_branch_b, w_out_b, npost)
        x2 = _ffn(l, x2, seq, fpre, ffn_w_in_b, ffn_conv_w, vec(ffn_conv_b), ffn_w_out_b, fpost)
    return x2.reshape(bsz, seq, D_MODEL)
```

```python
import functools
import math

import numpy as np
import jax
import jax.numpy as jnp
from jax import lax
from jax.experimental import pallas as pl
from jax.experimental.pallas import tpu as pltpu

F32 = jnp.float32
BF16 = jnp.bfloat16

D_MODEL = 1024
MIX = 256
NORM_EPS = 1e-6
S5_GROUPS, S5_GROUP, S5_STATE = 16, 16, 64
S5_NSTATE = S5_GROUPS * S5_STATE
S5_DT_MIN, S5_DT_MAX = 1e-3, 1e-1
RWKV_HEADS, RWKV_HEAD = 4, 64
RWKV_WIDTH = 3 * MIX + 128
RWKV_GN_EPS = 64e-5
GLA_HEADS, GLA_DK, GLA_DV = 4, 32, 64
GLA_KEY = GLA_HEADS * GLA_DK
GLA_GATE_LORA = 16
GLA_GATE_NORM = 16.0
GLA_WIDTH = 2 * GLA_KEY + 2 * MIX + GLA_GATE_LORA
GLA_WIDTH_PAD = 896
RG_CONV = 4
RG_C = 8.0
FFN_DIM = 2816
FFN_CONV = 3
HALO = 8

MIX_T = 128
S5_T = RG_T = GLA_T = MIX_T
RWKV_C = 64
SUB = 8
PROJ_T = 1024
MERGE_T = 512
FFN_T = 512

VMEM_LIMIT = 56 * 1024 * 1024


def _dot(a, b):
    return jnp.dot(a.astype(BF16), b.astype(BF16), preferred_element_type=F32)


def _dot_nt(a, b):
    return lax.dot_general(a.astype(BF16), b.astype(BF16), (((1,), (1,)), ((), ())),
                           preferred_element_type=F32)


def _dot_tn(a, b):
    return lax.dot_general(a.astype(BF16), b.astype(BF16), (((0,), (0,)), ((), ())),
                           preferred_element_type=F32)


def _split3(x):
    hi = x.astype(BF16)
    r1 = x - hi.astype(F32)
    mid = r1.astype(BF16)
    lo = (r1 - mid.astype(F32)).astype(BF16)
    return hi, mid, lo


def _sel_dot(c, x):
    hi, mid, lo = _split3(x)
    return (jnp.dot(c, hi, preferred_element_type=F32)
            + jnp.dot(c, mid, preferred_element_type=F32)
            + jnp.dot(c, lo, preferred_element_type=F32))


def _seg_dot(x, j):
    hi = x.astype(BF16)
    lo = (x - hi.astype(F32)).astype(BF16)
    m = x.shape[0]
    r = jnp.dot(jnp.concatenate([hi, lo], axis=0), j, preferred_element_type=F32)
    return r[0:m] + r[m:2 * m]


def _seg_dot_left(c, x):
    hi = x.astype(BF16)
    lo = (x - hi.astype(F32)).astype(BF16)
    n = x.shape[1]
    r = jnp.dot(c, jnp.concatenate([hi, lo], axis=1), preferred_element_type=F32)
    return r[:, 0:n] + r[:, n:2 * n]


def _rms(x, w):
    return x * lax.rsqrt(jnp.mean(x * x, axis=-1, keepdims=True) + NORM_EPS) * w


def _softplus(y):
    return jnp.maximum(y, 0.0) + jnp.log1p(jnp.exp(-jnp.abs(y)))


def _shift_rows(x, s, fill):
    rolled = pltpu.roll(x, s, axis=0)
    row = lax.broadcasted_iota(jnp.int32, x.shape, 0)
    return jnp.where(row >= s, rolled, fill)


def _stack_heads(x, hm_ref, nheads):
    xb = x.astype(BF16)
    return jnp.concatenate([xb * hm_ref[h:h + 1, :].astype(BF16) for h in range(nheads)], axis=0)


def _round_robin(gens):
    while gens:
        alive = []
        for gen in gens:
            try:
                next(gen)
                alive.append(gen)
            except StopIteration:
                pass
        gens = alive


def _const_spec(shape):
    nd = len(shape)
    return pl.BlockSpec(shape, lambda *_: (0,) * nd, pipeline_mode=pl.Buffered(1))


def _layer_spec(arr, l):
    nd = arr.ndim
    return pl.BlockSpec((None,) + arr.shape[1:], lambda *_: (l,) + (0,) * (nd - 1),
                        pipeline_mode=pl.Buffered(1))


def _sigmoid(x):
    return 0.5 * jnp.tanh(0.5 * x) + 0.5


def _params(sem):
    return pltpu.CompilerParams(dimension_semantics=sem, vmem_limit_bytes=VMEM_LIMIT)


def _proj_kernel(x_ref, nw_ref, *refs):
    n = len(refs) // 2
    h = _rms(x_ref[...], nw_ref[...]).astype(BF16)
    for w_ref, o_ref in zip(refs[:n], refs[n:]):
        o_ref[...] = jnp.dot(h, w_ref[...], preferred_element_type=F32)


def _proj(l, x2, nw, ws):
    n_tok = x2.shape[0]
    widths = [w.shape[-1] for w in ws]
    return pl.pallas_call(
        _proj_kernel,
        grid=(n_tok // PROJ_T,),
        in_specs=[pl.BlockSpec((PROJ_T, D_MODEL), lambda i: (i, 0)), _layer_spec(nw, l)]
        + [_layer_spec(w, l) for w in ws],
        out_specs=[pl.BlockSpec((PROJ_T, wd), lambda i: (i, 0)) for wd in widths],
        out_shape=[jax.ShapeDtypeStruct((n_tok, wd), F32) for wd in widths],
        compiler_params=_params(("parallel",)),
        name="proj",
    )(x2, nw, *ws)


def _cmul_add(hre, him, ar, ai, sre, sim):
    return hre + ar * sre - ai * sim, him + ar * sim + ai * sre


def _block_ends(x, nblk):
    return jnp.concatenate([x[j * SUB + SUB - 1:(j + 1) * SUB] for j in range(nblk)], axis=0)


S5_YIELD_EVERY = 8


def _s5_stream(u_ref, bmat_ref, are_ref, aim_ref, pwre_ref, pwim_ref, cmat_ref, d_ref, wglu_ref,
               bglu_ref, perm_ref, permt_ref, o_ref, h0_ref, hbuf_ref):
    bsz, t = u_ref.shape[0], S5_T
    nseg = SUB // bsz
    seglen = t // nseg
    rows = bsz * t
    ns = S5_NSTATE
    u = u_ref[...].reshape(rows, MIX)
    up = jnp.dot(perm_ref[...], u.astype(BF16), preferred_element_type=F32).astype(BF16)
    hbuf_ref[...] = jnp.dot(up, bmat_ref[...], preferred_element_type=F32)
    yield
    are = are_ref[...]
    aim = aim_ref[...]
    hre = h0_ref[0]
    him = h0_ref[1]
    for tt in range(seglen):
        r0 = tt * SUB
        hre, him = _cmul_add(hbuf_ref[r0:r0 + SUB, 0:ns], hbuf_ref[r0:r0 + SUB, ns:2 * ns],
                             are, aim, hre, him)
        hbuf_ref[r0:r0 + SUB, 0:ns] = hre
        hbuf_ref[r0:r0 + SUB, ns:2 * ns] = him
        if tt % S5_YIELD_EVERY == S5_YIELD_EVERY - 1:
            yield
    seg = lax.broadcasted_iota(jnp.int32, (SUB, ns), 0) & (nseg - 1)
    later = seg >= 1
    pre = pwre_ref[rows - SUB:rows, :]
    pim = pwim_ref[rows - SUB:rows, :]
    tre, tim = hre, him
    for _ in range(nseg - 1):
        sre = jnp.where(later, pltpu.roll(tre, 1, axis=0), 0.0)
        sim = jnp.where(later, pltpu.roll(tim, 1, axis=0), 0.0)
        tre, tim = _cmul_add(hre, him, pre, pim, sre, sim)
    cre = jnp.where(later, pltpu.roll(tre, 1, axis=0), 0.0)
    cim = jnp.where(later, pltpu.roll(tim, 1, axis=0), 0.0)
    first = seg == 0
    h0_ref[0] = jnp.where(first, pltpu.roll(tre, SUB - (nseg - 1), axis=0), 0.0)
    h0_ref[1] = jnp.where(first, pltpu.roll(tim, SUB - (nseg - 1), axis=0), 0.0)
    for tt in range(seglen):
        r0 = tt * SUB
        fre, fim = _cmul_add(hbuf_ref[r0:r0 + SUB, 0:ns], hbuf_ref[r0:r0 + SUB, ns:2 * ns],
                             pwre_ref[r0:r0 + SUB, :], pwim_ref[r0:r0 + SUB, :], cre, cim)
        hbuf_ref[r0:r0 + SUB, 0:ns] = fre
        hbuf_ref[r0:r0 + SUB, ns:2 * ns] = fim
        if tt % S5_YIELD_EVERY == S5_YIELD_EVERY - 1:
            yield
    yp = _dot(hbuf_ref[...], cmat_ref[...])
    yield
    y = _seg_dot_left(permt_ref[...], yp) + d_ref[...] * u
    yield
    y = jax.nn.gelu(y)
    y = y * jax.nn.sigmoid(_dot(y, wglu_ref[...]) + bglu_ref[...])
    o_ref[...] = y.reshape(bsz, t, MIX)


def _s5_shared(bsz):
    rows = bsz * S5_T
    seglen = rows // SUB
    src = (np.arange(rows) % SUB) * seglen + np.arange(rows) // SUB
    perm = np.zeros((rows, rows), np.float32)
    perm[np.arange(rows), src] = 1.0
    return [jnp.asarray(perm, BF16), jnp.asarray(perm.T, BF16)]


def _s5_layer_consts(bsz, lam_re, lam_im, log_dt, b_re, b_im, c_re, c_im, d, w_glu, b_glu):
    dt = jnp.exp(log_dt).reshape(S5_GROUPS, 1)
    mag = jnp.exp(lam_re * dt)
    ang = lam_im * dt
    ab_re = mag * jnp.cos(ang)
    ab_im = mag * jnp.sin(ang)
    den = lam_re * lam_re + lam_im * lam_im
    f_re = ((ab_re - 1.0) * lam_re + ab_im * lam_im) / den
    f_im = (ab_im * lam_re - (ab_re - 1.0) * lam_im) / den
    f_re = f_re.reshape(S5_GROUPS, S5_STATE, 1)
    f_im = f_im.reshape(S5_GROUPS, S5_STATE, 1)
    bb_re = f_re * b_re - f_im * b_im
    bb_im = f_re * b_im + f_im * b_re
    eye = jnp.eye(S5_GROUPS, dtype=F32)
    bm_re = jnp.einsum('gpc,gh->gchp', bb_re, eye).reshape(MIX, S5_NSTATE)
    bm_im = jnp.einsum('gpc,gh->gchp', bb_im, eye).reshape(MIX, S5_NSTATE)
    bmat = jnp.concatenate([bm_re, bm_im], axis=1).astype(BF16)
    cm_re = jnp.einsum('gcp,gh->gphc', c_re, eye).reshape(S5_NSTATE, MIX)
    cm_im = jnp.einsum('gcp,gh->gphc', c_im, eye).reshape(S5_NSTATE, MIX)
    cmat = jnp.concatenate([cm_re, -cm_im], axis=0).astype(BF16)
    assert SUB % bsz == 0
    seglen = S5_T * bsz // SUB
    pw_re = ab_re.reshape(1, S5_NSTATE)
    pw_im = ab_im.reshape(1, S5_NSTATE)
    while pw_re.shape[0] < seglen:
        top_re, top_im = pw_re[-1:], pw_im[-1:]
        pw_re, pw_im = (jnp.concatenate([pw_re, pw_re * top_re - pw_im * top_im], axis=0),
                        jnp.concatenate([pw_im, pw_re * top_im + pw_im * top_re], axis=0))
    return (bmat, pw_re, pw_im, cmat, d.reshape(1, MIX), w_glu.astype(BF16), b_glu.reshape(1, MIX))


def _rg_streams(p_ref, cw_ref, cb_ref, wa_ref, ba_ref, wx_ref, bx_ref, sp_ref,
                o_ref, xbuf_ref, h_ref):
    first = pl.program_id(0) == 0
    t, nblk = RG_T, RG_T // SUB
    row = lax.broadcasted_iota(jnp.int32, (t, MIX), 0)
    sub_row = lax.broadcasted_iota(jnp.int32, (nblk, SUB, MIX), 1)

    def sequence(bi):
        p = p_ref[bi]
        x = p[:, :MIX]
        y = p[:, MIX:]
        xbuf_ref[bi, HALO:HALO + t, :] = x
        xc = cb_ref[...] + cw_ref[RG_CONV - 1:RG_CONV, :] * x
        for j in range(1, RG_CONV):
            xc = xc + (cw_ref[RG_CONV - 1 - j:RG_CONV - j, :]
                       * xbuf_ref[bi, HALO - j:HALO - j + t, :])
        xbuf_ref[bi, 0:HALO, :] = x[t - HALO:t, :]
        za = _dot(xc, wa_ref[...])
        zx = _dot(xc, wx_ref[...])
        yield
        gate_r = jax.nn.sigmoid(za + ba_ref[...])
        gate_i = jax.nn.sigmoid(zx + bx_ref[...])
        log_a = -RG_C * gate_r * sp_ref[...]
        a = jnp.exp(log_a)
        th = jnp.tanh(log_a)
        mult = jnp.sqrt(-2.0 * th / (1.0 - th))
        mult = jnp.where(jnp.logical_and(first, row == 0), 1.0, mult)
        b = mult * (gate_i * xc)
        a = a.reshape(nblk, SUB, MIX)
        b = b.reshape(nblk, SUB, MIX)
        for i in range(int(math.log2(SUB))):
            live = sub_row >= 2 ** i
            b = b + a * jnp.where(live, pltpu.roll(b, 2 ** i, axis=1), 0.0)
            a = a * jnp.where(live, pltpu.roll(a, 2 ** i, axis=1), 1.0)
        a = a.reshape(t, MIX)
        b = b.reshape(t, MIX)
        yield
        ea = _block_ends(a, nblk)
        eb = _block_ends(b, nblk)
        s = 1
        while s < nblk:
            eb = eb + ea * _shift_rows(eb, s, 0.0)
            ea = ea * _shift_rows(ea, s, 1.0)
            s *= 2
        h_in = h_ref[bi]
        c = _shift_rows(eb, 1, 0.0) + _shift_rows(ea, 1, 1.0) * h_in
        h_ref[bi] = eb[nblk - 1:nblk] + ea[nblk - 1:nblk] * h_in
        hs = [b[j * SUB:(j + 1) * SUB] + a[j * SUB:(j + 1) * SUB] * c[j:j + 1] for j in range(nblk)]
        o_ref[bi] = jnp.concatenate(hs, axis=0) * jax.nn.gelu(y)

    return [sequence(bi) for bi in range(p_ref.shape[0])]


def _block_diag(w):
    nh, n, _ = w.shape
    eye = jnp.eye(nh, dtype=w.dtype)
    return jnp.einsum('hij,hg->higj', w, eye).reshape(nh * n, nh * n)


def _rg_layer_consts(conv_w, conv_b, w_a, b_a, w_x, b_x, lam):
    sp = jax.nn.softplus(-lam).reshape(1, MIX)
    row = lambda t: t.reshape(1, MIX)
    return (conv_w, row(conv_b), _block_diag(w_a).astype(BF16), row(b_a),
            _block_diag(w_x).astype(BF16), row(b_x), sp)


GLA_LEVELS = int(math.log2(GLA_T)) + 1


def _gla_consts():
    t = GLA_T
    tri = np.tril(np.ones((t, t), np.float32))
    cs = [tri]
    masks = [np.eye(t, dtype=np.float32)]
    idx = np.arange(t)
    for lvl in range(1, GLA_LEVELS):
        s = 2 ** (lvl - 1)
        mid = (idx // (2 * s)) * (2 * s) + s
        cs.append(tri[mid - 1] - tri)
        masks.append(((idx[:, None] ^ idx[None, :]) < 2 * s).astype(np.float32))
    cs.append(tri[t - 1:t] - tri)
    hm_k = np.zeros((GLA_HEADS, GLA_KEY), np.float32)
    hm_v = np.zeros((GLA_HEADS, MIX), np.float32)
    for h in range(GLA_HEADS):
        hm_k[h, h * GLA_DK:(h + 1) * GLA_DK] = 1.0
        hm_v[h, h * GLA_DV:(h + 1) * GLA_DV] = 1.0
    bd = hm_v.T @ hm_k
    seg = (hm_v.T @ hm_v) / GLA_DV
    return (jnp.asarray(np.concatenate(cs, axis=0), BF16), jnp.asarray(np.stack(masks), F32),
            jnp.asarray(hm_k), jnp.asarray(hm_v), jnp.asarray(bd), jnp.asarray(seg, BF16))


def _gla_streams(p_ref, aup_ref, ab_ref, nw_ref, cs_ref, mask_ref, hmk_ref, hmv_ref, bd_ref,
                 seg_ref, o_ref, st_ref):
    t = GLA_T
    row = lax.broadcasted_iota(jnp.int32, (t, GLA_KEY), 0)

    def sequence(b):
        p = p_ref[b]
        q = p[:, 0:GLA_KEY] * (GLA_DK ** -0.5)
        k = p[:, GLA_KEY:2 * GLA_KEY]
        v = p[:, 2 * GLA_KEY:2 * GLA_KEY + MIX]
        g = p[:, 2 * GLA_KEY + MIX:2 * GLA_KEY + 2 * MIX]
        lora = p[:, 2 * GLA_KEY + 2 * MIX:]
        z = _dot(lora, aup_ref[...]) + ab_ref[...]
        yield
        gk = -_softplus(-z) / GLA_GATE_NORM
        br = _sel_dot(cs_ref[...], gk)
        yield
        bc = br[0:t]
        st = st_ref[b]
        o_inter = _dot_nt(q * jnp.exp(bc), st)
        kd = k * jnp.exp(br[(GLA_LEVELS) * t:(GLA_LEVELS + 1) * t])
        st_ref[b] = st * jnp.exp(bc[t - 1:t, :]) + _dot_tn(v, kd) * bd_ref[...]
        attn = [None] * GLA_HEADS
        for lvl in range(GLA_LEVELS):
            if lvl == 0:
                qt, kt = q, k
            else:
                e = jnp.exp(-jnp.abs(br[lvl * t:(lvl + 1) * t]))
                upper = (row & (2 ** (lvl - 1))) != 0
                qt = jnp.where(upper, q * e, 0.0)
                kt = jnp.where(upper, 0.0, k * e)
            pr = _dot_nt(_stack_heads(qt, hmk_ref, GLA_HEADS), kt)
            yield
            m = mask_ref[lvl]
            for h in range(GLA_HEADS):
                term = pr[h * t:(h + 1) * t] * m
                attn[h] = term if attn[h] is None else attn[h] + term
        ov = _dot(jnp.concatenate(attn, axis=0), v)
        yield
        o = o_inter + ov[0:t] * hmv_ref[0:1, :]
        for h in range(1, GLA_HEADS):
            o = o + ov[h * t:(h + 1) * t] * hmv_ref[h:h + 1, :]
        ms = _seg_dot(o * o, seg_ref[...])
        yield
        o = o * lax.rsqrt(ms + NORM_EPS)
        o_ref[b] = o * nw_ref[...] * (g * jax.nn.sigmoid(g))

    return [sequence(b) for b in range(p_ref.shape[0])]


def _gla_layer_consts(alpha_up, alpha_b, norm_w):
    lora_tile = GLA_WIDTH_PAD - 2 * GLA_KEY - 2 * MIX
    aup = jnp.pad(alpha_up, ((0, lora_tile - GLA_GATE_LORA), (0, 0))).astype(BF16)
    return aup, alpha_b.reshape(1, GLA_KEY), norm_w.reshape(1, MIX)


def _rwkv_consts():
    c, nh, hd = RWKV_C, RWKV_HEADS, RWKV_HEAD
    hm = np.zeros((nh, MIX), np.float32)
    for h in range(nh):
        hm[h, h * hd:(h + 1) * hd] = 1.0
    tri = np.tril(np.ones((c, c), np.float32))
    eye = np.eye(c, dtype=np.float32)
    incl = np.tile(tri, (1, nh))
    strict = np.tile(tri - eye, (1, nh))
    idx = np.arange(c)
    levels = []
    for lvl in range(int(math.log2(c))):
        s = 2 ** lvl
        m = (((idx[:, None] ^ idx[None, :]) < 2 * s) & ((idx[:, None] & s) != 0)
             & ((idx[None, :] & s) == 0))
        levels.append(np.tile(m.astype(np.float32), (1, nh)))
    ident = np.stack([np.tile(eye, (1, nh))] + levels)
    seg = hm.T @ hm
    return (jnp.asarray(hm), jnp.asarray(tri, BF16), jnp.asarray(strict), jnp.asarray(incl),
            jnp.asarray(ident), jnp.asarray(seg, BF16), jnp.asarray(seg))


def _rwkv_streams(p_ref, mu_ref, w0_ref, wup_ref, a0_ref, aup_ref, gup_ref, kk_ref, ka_ref,
                  rk_ref, lnw_ref, lnb_ref, hm_ref, tri_ref, strict_ref, incl_ref, ident_ref,
                  seg_ref, bd_ref, o_ref, prev_ref, mt_ref):
    c, nh = RWKV_C, RWKV_HEADS
    n = nh * c
    row = lax.broadcasted_iota(jnp.int32, (c, RWKV_WIDTH), 0)

    def sequence(b):
        for r0 in range(0, p_ref.shape[1], c):
            yield from chunk(b, r0)

    def chunk(b, r0):
        p0 = p_ref[b, r0:r0 + c, :]
        shifted = jnp.where(row == 0, prev_ref[b], pltpu.roll(p0, 1, axis=0))
        prev_ref[b] = p0[c - 1:c, :]
        p = p0 + (shifted - p0) * mu_ref[...]
        r = p[:, 0:MIX]
        k = p[:, MIX:2 * MIX]
        v = p[:, 2 * MIX:3 * MIX]
        lora = p[:, 3 * MIX:]
        w = -_softplus(-(w0_ref[...] + _dot(jnp.tanh(lora), wup_ref[...]))) - 0.5
        wlog = -jnp.exp(w)
        a = jax.nn.sigmoid(a0_ref[...] + _dot(lora, aup_ref[...]))
        g = _dot(jax.nn.sigmoid(lora), gup_ref[...])
        yield
        kk = k * kk_ref[...]
        k = k * (1.0 + (a - 1.0) * ka_ref[...])
        sums = _seg_dot(jnp.concatenate([kk * kk, r * k * rk_ref[...]], axis=0), seg_ref[...])
        cl = _sel_dot(tri_ref[...], wlog)
        yield
        kk = kk / jnp.maximum(jnp.sqrt(sums[0:c]), 1e-12)
        bonus = sums[c:2 * c] * v
        bvec = kk * a
        g_inv = jnp.exp(-cl)
        at = -kk * jnp.exp(cl - wlog)
        rt = r * jnp.exp(cl)
        x = jnp.concatenate([at, rt], axis=0)
        y = jnp.concatenate([_stack_heads(bvec * g_inv, hm_ref, nh),
                             _stack_heads(k * g_inv, hm_ref, nh)], axis=0)
        pr = _dot_nt(x, y)
        mt = mt_ref[b]
        am = _dot_nt(x, mt)
        yield
        strict = strict_ref[...]
        incl = incl_ref[...]
        nab = pr[0:c, 0:n]
        aak = pr[0:c, n:2 * n] * strict
        arb = pr[c:2 * c, 0:n] * incl
        ark = pr[c:2 * c, n:2 * n] * incl
        vbd = _stack_heads(v, hm_ref, nh)
        rhs = _dot(aak, vbd) + am[0:c]
        tm = ident_ref[0]
        for lvl in range(int(math.log2(c))):
            x = nab * ident_ref[1 + lvl]
            if lvl > 0:
                x = _dot(x, _stack_heads(tm, hm_ref, nh))
                yield
                x = _dot(tm, _stack_heads(x, hm_ref, nh))
                yield
            tm = tm + x
        u = _dot(tm, _stack_heads(rhs, hm_ref, nh))
        yield
        o = am[c:2 * c] + _dot(jnp.concatenate([arb, ark], axis=1),
                               jnp.concatenate([_stack_heads(u, hm_ref, nh), vbd], axis=0))
        cl_last = cl[c - 1:c, :]
        tail = jnp.exp(cl_last - cl)
        upd = _dot_tn(jnp.concatenate([u, v], axis=0),
                      jnp.concatenate([bvec * tail, k * tail], axis=0))
        yield
        mt_ref[b] = mt * jnp.exp(cl_last) + upd * bd_ref[...]
        inv_n = 1.0 / RWKV_HEAD
        mean = _seg_dot(o, seg_ref[...]) * inv_n
        yield
        oc = o - mean
        var = _seg_dot(oc * oc, seg_ref[...]) * inv_n
        yield
        on = oc * lax.rsqrt(var + RWKV_GN_EPS) * lnw_ref[...] + lnb_ref[...]
        o_ref[b, r0:r0 + c, :] = (on + bonus) * g

    return [sequence(b) for b in range(p_ref.shape[0])]


def _rwkv_layer_consts(mu, w0, w_up, a0, a_up, g_up, k_k, k_a, r_k, ln_w, ln_b):
    row = lambda t: t.reshape(1, -1)
    place = lambda w, lo: jnp.pad(w, ((lo, 128 - lo - w.shape[0]), (0, 0))).astype(BF16)
    return (row(mu), row(w0), place(w_up, 0), row(a0), place(a_up, 32), place(g_up, 64), row(k_k),
            row(k_a), row(r_k), row(ln_w), row(ln_b))


def _mixers_kernel(*refs, counts):
    n_s5, n_rwkv, n_gla, n_rg = counts
    it = iter(refs)
    take = lambda k: [next(it) for _ in range(k)]
    u_ref, p_rwkv_ref, p_gla_ref, p_rg_ref = take(4)
    c_s5, c_rwkv, c_gla, c_rg = take(n_s5), take(n_rwkv), take(n_gla), take(n_rg)
    ya_ref, yb_ref, yc_ref, yd_ref = take(4)
    s_s5, s_rwkv, s_gla, s_rg = take(2), take(2), take(1), take(2)
    state_refs = [s_s5[0]] + s_rwkv + s_gla + s_rg

    @pl.when(pl.program_id(0) == 0)
    def _():
        for ref in state_refs:
            ref[...] = jnp.zeros_like(ref)

    rwkv = _rwkv_streams(p_rwkv_ref, *c_rwkv, yb_ref, *s_rwkv)
    gla = _gla_streams(p_gla_ref, *c_gla, yc_ref, *s_gla)
    rg = _rg_streams(p_rg_ref, *c_rg, yd_ref, *s_rg)
    s5 = [_s5_stream(u_ref, *c_s5, ya_ref, *s_s5)]
    gens = s5 + rwkv[:2] + rg[:2] + gla[:2] + rwkv[2:] + rg[2:] + gla[2:]
    _round_robin(gens)


def _mixer_layer_consts(bsz, s5_args, rwkv_args, gla_args, rg_args):
    bmat, pw_re, pw_im, *rest = jax.vmap(functools.partial(_s5_layer_consts, bsz))(*s5_args)
    pw_re = jnp.repeat(pw_re, SUB, axis=1)
    pw_im = jnp.repeat(pw_im, SUB, axis=1)
    s5 = (bmat, pw_re[:, 0:SUB], pw_im[:, 0:SUB], pw_re, pw_im, *rest)
    return (s5,
            jax.vmap(_rwkv_layer_consts)(*rwkv_args),
            jax.vmap(_gla_layer_consts)(*gla_args),
            jax.vmap(_rg_layer_consts)(*rg_args))


def _mixers(l, acts, layer_consts):
    bsz, seq, _ = acts[0].shape
    t = MIX_T
    shared = (_s5_shared(bsz), list(_rwkv_consts()), list(_gla_consts()), [])
    specs, operands, counts = [], [], []
    for lay, sh in zip(layer_consts, shared):
        specs += [_layer_spec(c, l) for c in lay] + [_const_spec(c.shape) for c in sh]
        operands += list(lay) + sh
        counts.append(len(lay) + len(sh))
    scratch = [pltpu.VMEM((2, SUB, S5_NSTATE), F32), pltpu.VMEM((bsz * t, 2 * S5_NSTATE), F32),
               pltpu.VMEM((bsz, 1, RWKV_WIDTH), F32), pltpu.VMEM((bsz, MIX, MIX), F32),
               pltpu.VMEM((bsz, MIX, GLA_KEY), F32),
               pltpu.VMEM((bsz, HALO + RG_T, MIX), F32), pltpu.VMEM((bsz, 1, MIX), F32)]
    seq_spec = lambda w: pl.BlockSpec((bsz, t, w), lambda i: (0, i, 0))
    return pl.pallas_call(
        functools.partial(_mixers_kernel, counts=tuple(counts)),
        grid=(seq // t,),
        in_specs=[seq_spec(a.shape[-1]) for a in acts] + specs,
        out_specs=[seq_spec(MIX)] * 4,
        out_shape=[jax.ShapeDtypeStruct((bsz, seq, MIX), F32)] * 4,
        scratch_shapes=scratch,
        compiler_params=_params(("arbitrary",)),
        name="mixers",
    )(*acts, *operands)


def _merge_kernel(x_ref, ya_ref, yb_ref, yc_ref, yd_ref, npre_ref, wg_ref, wb_ref, wo_ref,
                  npost_ref, o_ref):
    x = x_ref[...]
    h = _rms(x, npre_ref[...]).astype(BF16)
    merged = None
    for kbr, y_ref in enumerate((ya_ref, yb_ref, yc_ref, yd_ref)):
        gate = _sigmoid(jnp.dot(h, wg_ref[:, kbr * D_MODEL:(kbr + 1) * D_MODEL],
                                preferred_element_type=F32))
        term = gate * _dot(y_ref[...], wb_ref[kbr])
        merged = term if merged is None else merged + term
    m = _dot(merged, wo_ref[...])
    o_ref[...] = x + _rms(m, npost_ref[...])


def _merge(l, x2, ys, npre, wg, wb, wo, npost):
    n_tok = x2.shape[0]
    t = MERGE_T
    return pl.pallas_call(
        _merge_kernel,
        grid=(n_tok // t,),
        in_specs=[pl.BlockSpec((t, D_MODEL), lambda i: (i, 0))]
        + [pl.BlockSpec((t, MIX), lambda i: (i, 0))] * 4
        + [_layer_spec(a, l) for a in (npre, wg, wb, wo, npost)],
        out_specs=pl.BlockSpec((t, D_MODEL), lambda i: (i, 0)),
        out_shape=jax.ShapeDtypeStruct((n_tok, D_MODEL), F32),
        compiler_params=_params(("parallel",)),
        name="merge",
    )(x2, *ys, npre, wg, wb, wo, npost)


def _ffn_kernel(x_ref, npre_ref, win_ref, cw_ref, cb_ref, wout_ref, npost_ref, o_ref, gbuf_ref,
                *, tiles_per_seq):
    @pl.when(pl.program_id(0) % tiles_per_seq == 0)
    def _():
        gbuf_ref[0:HALO, :] = jnp.zeros((HALO, FFN_DIM), F32)

    t = FFN_T
    x = x_ref[...]
    h = _rms(x, npre_ref[...]).astype(BF16)
    gate = jnp.dot(h, win_ref[:, :FFN_DIM], preferred_element_type=F32)
    val = jnp.dot(h, win_ref[:, FFN_DIM:], preferred_element_type=F32)
    gbuf_ref[HALO:HALO + t, :] = gate
    gc = cb_ref[...] + cw_ref[FFN_CONV - 1:FFN_CONV, :] * gate
    for j in range(1, FFN_CONV):
        gc = gc + cw_ref[FFN_CONV - 1 - j:FFN_CONV - j, :] * gbuf_ref[HALO - j:HALO - j + t, :]
    gbuf_ref[0:HALO, :] = gate[t - HALO:t, :]
    f = _dot(jax.nn.gelu(gc) * val, wout_ref[...])
    o_ref[...] = x + _rms(f, npost_ref[...])


def _ffn(l, x2, seq, npre, w_in, conv_w, conv_b, w_out, npost):
    n_tok = x2.shape[0]
    t = FFN_T
    return pl.pallas_call(
        functools.partial(_ffn_kernel, tiles_per_seq=seq // t),
        grid=(n_tok // t,),
        in_specs=[pl.BlockSpec((t, D_MODEL), lambda i: (i, 0))]
        + [_layer_spec(a, l) for a in (npre, w_in, conv_w, conv_b, w_out, npost)],
        out_specs=pl.BlockSpec((t, D_MODEL), lambda i: (i, 0)),
        out_shape=jax.ShapeDtypeStruct((n_tok, D_MODEL), F32),
        scratch_shapes=[pltpu.VMEM((HALO + t, FFN_DIM), F32)],
        compiler_params=_params(("arbitrary",)),
        name="convffn",
    )(x2, npre, w_in, conv_w, conv_b, w_out, npost)


def kernel(x, norm_mix_pre, norm_mix_post, norm_ffn_pre, norm_ffn_post, w_in, s5_lambda_re, s5_lambda_im, s5_log_dt, s5_b_re, s5_b_im, s5_c_re, s5_c_im, s5_d, s5_w_glu, s5_b_glu, rwkv_mu, rwkv_w0, rwkv_w_up, rwkv_a0, rwkv_a_up, rwkv_g_up, rwkv_k_k, rwkv_k_a, rwkv_r_k, rwkv_ln_w, rwkv_ln_b, gla_alpha_up, gla_alpha_b, gla_norm_w, rg_conv_w, rg_conv_b, rg_w_a, rg_b_a, rg_w_x, rg_b_x, rg_lambda, w_branch, w_out, ffn_w_in, ffn_conv_w, ffn_conv_b, ffn_w_out):
    bsz, seq, _ = x.shape
    depth = w_in.shape[0]
    n_tok = bsz * seq
    x2 = x.reshape(n_tok, D_MODEL)
    o_rwkv = MIX
    o_gla = o_rwkv + 3 * MIX + 128
    o_rg = o_gla + GLA_WIDTH
    o_gate = o_rg + 2 * MIX
    w_in_b = w_in.astype(BF16)
    w_mix = [w_in_b[:, :, :o_rwkv], w_in_b[:, :, o_rwkv:o_gla],
             lax.pad(w_in_b[:, :, o_gla:o_rg], jnp.zeros((), BF16),
                     [(0, 0, 0), (0, 0, 0), (0, GLA_WIDTH_PAD - GLA_WIDTH, 0)]),
             w_in_b[:, :, o_rg:o_gate]]
    w_gate = w_in_b[:, :, o_gate:]
    w_branch_b, w_out_b = w_branch.astype(BF16), w_out.astype(BF16)
    ffn_w_in_b, ffn_w_out_b = ffn_w_in.astype(BF16), ffn_w_out.astype(BF16)
    vec = lambda t: t.reshape(depth, 1, t.shape[-1])
    npre, npost, fpre, fpost = (vec(norm_mix_pre), vec(norm_mix_post), vec(norm_ffn_pre),
                                vec(norm_ffn_post))
    mixer_consts = _mixer_layer_consts(
        bsz,
        (s5_lambda_re, s5_lambda_im, s5_log_dt, s5_b_re, s5_b_im, s5_c_re, s5_c_im, s5_d, s5_w_glu,
         s5_b_glu),
        (rwkv_mu, rwkv_w0, rwkv_w_up, rwkv_a0, rwkv_a_up, rwkv_g_up, rwkv_k_k, rwkv_k_a, rwkv_r_k,
         rwkv_ln_w, rwkv_ln_b),
        (gla_alpha_up, gla_alpha_b, gla_norm_w),
        (rg_conv_w, rg_conv_b, rg_w_a, rg_b_a, rg_w_x, rg_b_x, rg_lambda))
    for l in range(depth):
        ps = _proj(l, x2, npre, w_mix)
        ys = _mixers(l, [t.reshape(bsz, seq, t.shape[-1]) for t in ps], mixer_consts)
        ys = [t.reshape(n_tok, MIX) for t in ys]
        x2 = _merge(l, x2, ys, npre, w_gate, w_branch_b, w_out_b, npost)
        x2 = _ffn(l, x2, seq, fpre, ffn_w_in_b, ffn_conv_w, vec(ffn_conv_b), ffn_w_out_b, fpost)
    return x2.reshape(bsz, seq, D_MODEL)
```
